```python
import math
import jax, jax.numpy as jnp
from jax import lax
import numpy as np

D_MODEL = 1024
BATCH = 8
SEQ = 4096
DEPTH = 1

HEAD_DIM = 64
ATT_WIDTH = D_MODEL
CONV_WIDTH = D_MODEL
MIX_WIDTH = ATT_WIDTH + CONV_WIDTH
N_Q_HEADS = ATT_WIDTH // HEAD_DIM
N_KV_HEADS = 4
Q_PER_KV = N_Q_HEADS // N_KV_HEADS
KV_WIDTH = N_KV_HEADS * HEAD_DIM
N_CONV_GROUPS = CONV_WIDTH // HEAD_DIM
CONV_K = 31
DILATED_PATTERNS = ((128, 1), (512, 4), (2048, 16))
BLK = 128
NORM_EPS = 1e-6
LN_EPS = 1e-5
SPLIT_SIZES = (ATT_WIDTH, KV_WIDTH, KV_WIDTH, ATT_WIDTH, CONV_WIDTH, CONV_WIDTH, CONV_WIDTH)
IN_COLS = sum(SPLIT_SIZES)

kernel_name = "hybrid_dilated_attn_conformer_conv"


def rmsnorm(x, g):
    xf = x.astype(jnp.float32)
    y = xf * lax.rsqrt(jnp.mean(xf * xf, axis=-1, keepdims=True) + NORM_EPS)
    return (y * g.astype(jnp.float32)).astype(x.dtype)


def layernorm(x, g, b):
    xf = x.astype(jnp.float32)
    mu = jnp.mean(xf, axis=-1, keepdims=True)
    var = jnp.mean(jnp.square(xf - mu), axis=-1, keepdims=True)
    y = (xf - mu) * lax.rsqrt(var + LN_EPS)
    return (y * g.astype(jnp.float32) + b.astype(jnp.float32)).astype(x.dtype)


def alibi_slopes(n):
    return jnp.exp2(-8.0 * (jnp.arange(n, dtype=jnp.float32) + 1.0) / n)


def _to_blocks(t, dilation, n_blocks):
    b, s = t.shape[:2]
    rest = t.shape[2:]
    sub_len = s // dilation
    t = t.reshape((b, sub_len, dilation) + rest)
    t = jnp.moveaxis(t, 2, 1)
    t = jnp.pad(t, [(0, 0), (0, 0), (0, n_blocks * BLK - sub_len)] + [(0, 0)] * len(rest))
    return t.reshape((b, dilation, n_blocks, BLK) + rest)


def _from_blocks(t, seq):
    b, d, nb = t.shape[:3]
    rest = t.shape[4:]
    sub_len = seq // d
    t = t.reshape((b, d, nb * BLK) + rest)[:, :, :sub_len]
    t = jnp.moveaxis(t, 1, 2)
    return t.reshape((b, seq) + rest)


def _with_prev_block(t):
    prev = jnp.pad(t, [(0, 0), (0, 0), (1, 0)] + [(0, 0)] * (t.ndim - 3))[:, :, :-1]
    return jnp.concatenate([prev, t], axis=3)


def dilated_window_attention(q, k, v, slopes, window, dilation):
    seq = q.shape[1]
    sub_len = seq // dilation
    w = window // dilation
    nb = -(-sub_len // BLK)
    qb = _to_blocks(q, dilation, nb)
    kw = _with_prev_block(_to_blocks(k, dilation, nb))
    vw = _with_prev_block(_to_blocks(v, dilation, nb))
    s = jnp.einsum('brnqhgc,brnkhc->brnhgqk', qb, kw).astype(jnp.float32)
    qi = jnp.arange(BLK)[:, None]
    kj = jnp.arange(2 * BLK)[None, :]
    dist = BLK + qi - kj
    kpos = (jnp.arange(nb)[:, None, None] - 1) * BLK + kj
    valid = (dist >= 0) & (dist <= w) & (kpos >= 0)
    bias = -slopes[:, :, None, None] * (dist * dilation).astype(jnp.float32)
    s = jnp.where(valid[:, None, None], s + bias, -jnp.inf)
    m = jnp.max(s, axis=-1, keepdims=True)
    p = jnp.exp(s - m)
    l = jnp.sum(p, axis=-1)
    o = jnp.einsum('brnhgqk,brnkhc->brnqhgc', p, vw.astype(jnp.float32))
    l_q = jnp.moveaxis(l, -1, 3)
    o = o / l_q[..., None]
    lse = jnp.moveaxis(m[..., 0], -1, 3) + jnp.log(l_q)
    return _from_blocks(o, seq), _from_blocks(lse, seq)


def attention_branch(q, k, v, gate):
    b, s, _ = q.shape
    q = q.reshape(b, s, N_KV_HEADS, Q_PER_KV, HEAD_DIM) * (HEAD_DIM ** -0.5)
    k = k.reshape(b, s, N_KV_HEADS, HEAD_DIM)
    v = v.reshape(b, s, N_KV_HEADS, HEAD_DIM)
    slopes = alibi_slopes(N_Q_HEADS).reshape(N_KV_HEADS, Q_PER_KV)
    outs, lses = [], []
    for window, dilation in DILATED_PATTERNS:
        o, lse = dilated_window_attention(q, k, v, slopes, window, dilation)
        outs.append(o)
        lses.append(lse)
    wts = jax.nn.softmax(jnp.stack(lses, axis=0), axis=0)
    o = jnp.sum(wts[..., None] * jnp.stack(outs, axis=0), axis=0)
    o = o.reshape(b, s, ATT_WIDTH).astype(gate.dtype)
    return o * jax.nn.silu(gate)


def conv_branch(val, glu_gate, gate, conv_w, conv_b, ln_g, ln_b):
    h = val * jax.nn.sigmoid(glu_gate)
    h = lax.conv_general_dilated(
        h, conv_w.astype(h.dtype)[:, None, :], window_strides=(1,),
        padding=[(CONV_K - 1, 0)], dimension_numbers=('NWC', 'WIO', 'NWC'),
        feature_group_count=CONV_WIDTH) + conv_b.astype(h.dtype)
    h = layernorm(h, ln_g, ln_b)
    h = jax.nn.silu(h)
    return h * jax.nn.silu(gate)


def setup_inputs(seed: int = 0) -> dict:
    key = jax.random.key(seed)
    ks = jax.random.split(key, 9)
    f32 = jnp.float32
    x = jax.random.normal(ks[0], (BATCH, SEQ, D_MODEL), f32)
    norm_g = 1.0 + 0.02 * jax.random.normal(ks[1], (DEPTH, D_MODEL), f32)
    w_in = jax.random.normal(ks[2], (DEPTH, D_MODEL, IN_COLS), f32) * D_MODEL ** -0.5
    conv_w = jax.random.normal(ks[3], (DEPTH, CONV_K, CONV_WIDTH), f32) * CONV_K ** -0.5
    conv_b = 0.02 * jax.random.normal(ks[4], (DEPTH, CONV_WIDTH), f32)
    conv_ln_g = 1.0 + 0.02 * jax.random.normal(ks[5], (DEPTH, CONV_WIDTH), f32)
    conv_ln_b = 0.02 * jax.random.normal(ks[6], (DEPTH, CONV_WIDTH), f32)
    w_out = jax.random.normal(ks[7], (DEPTH, MIX_WIDTH, D_MODEL), f32) * MIX_WIDTH ** -0.5
    final_norm_g = 1.0 + 0.02 * jax.random.normal(ks[8], (D_MODEL,), f32)
    return {"x": x, "norm_g": norm_g, "w_in": w_in, "conv_w": conv_w, "conv_b": conv_b,
            "conv_ln_g": conv_ln_g, "conv_ln_b": conv_ln_b, "w_out": w_out,
            "final_norm_g": final_norm_g}


def reference(x, norm_g, w_in, conv_w, conv_b, conv_ln_g, conv_ln_b, w_out, final_norm_g):
    split_idx = list(np.cumsum(SPLIT_SIZES)[:-1])
    for layer in range(DEPTH):
        h = rmsnorm(x, norm_g[layer])
        proj = jnp.einsum('bsd,de->bse', h, w_in[layer])
        q, k, v, a_gate, c_val, c_glu, c_gate = jnp.split(proj, split_idx, axis=-1)
        y_att = attention_branch(q, k, v, a_gate)
        y_conv = conv_branch(c_val, c_glu, c_gate, conv_w[layer], conv_b[layer],
                             conv_ln_g[layer], conv_ln_b[layer])
        y = jnp.concatenate([y_att, y_conv], axis=-1)
        x = x + jnp.einsum('bse,ed->bsd', y, w_out[layer])
    return rmsnorm(x, final_norm_g)
```

```python
import functools

import jax
import jax.numpy as jnp
import numpy as np
from jax import lax
from jax.experimental import pallas as pl
from jax.experimental.pallas import tpu as pltpu

F32 = jnp.float32
BF16 = jnp.bfloat16

D_MODEL = 1024
HEAD_DIM = 64
N_Q_HEADS = 16
N_KV_HEADS = 4
Q_PER_KV = N_Q_HEADS // N_KV_HEADS
KV_WIDTH = N_KV_HEADS * HEAD_DIM
CONV_K = 31
WINDOW_STEPS = 128
NORM_EPS = 1e-6
LN_EPS = 1e-5

N_SLABS = 16
TILE_ROWS = 128
PATTERNS = ((1, 16, 8), (4, 4, 32), (16, 1, 128))
GROUP_Q = Q_PER_KV * HEAD_DIM
GROUP_KV = 2 * HEAD_DIM

K1_SLABS = 4
K3_SLABS = 4
CONV_ROWS = 64
CONV_TILE = 16
PREV_ROWS = 16
VMEM_LIMIT = 56 * 1024 * 1024


def _sigmoid(x):
    return 1.0 / (1.0 + jnp.exp(-x))


def _silu(x):
    return x * _sigmoid(x)


def _inproj_kernel(x_ref, g_ref, w_ref, q_ref, kv_ref, ag_ref, cv_ref, cu_ref, cg_ref):
    hs = []
    for s in range(K1_SLABS):
        xs = x_ref[:, s * D_MODEL:(s + 1) * D_MODEL]
        ms = jnp.mean(xs * xs, axis=-1, keepdims=True)
        hs.append((xs * lax.rsqrt(ms + NORM_EPS) * g_ref[...]).astype(BF16))
    h = jnp.concatenate(hs, axis=0)

    def emit(ref, c0, width, scale):
        for cc in range(0, width, 512):
            wc = min(512, width - cc)
            r = jnp.dot(h, w_ref[:, c0 + cc:c0 + cc + wc], preferred_element_type=F32)
            if scale is not None:
                r = r * scale
            for s in range(K1_SLABS):
                ref[s, :, cc:cc + wc] = r[s * TILE_ROWS:(s + 1) * TILE_ROWS].astype(ref.dtype)

    c = 0
    emit(q_ref, c, D_MODEL, HEAD_DIM ** -0.5)
    c += D_MODEL
    emit(kv_ref, c, 2 * KV_WIDTH, None)
    c += 2 * KV_WIDTH
    for ref in (ag_ref, cv_ref, cu_ref, cg_ref):
        emit(ref, c, D_MODEL, None)
        c += D_MODEL


def _inproj(x3, norm_g, w_perm):
    b = x3.shape[0]
    n_tiles = x3.shape[1] // TILE_ROWS
    rows = x3.shape[1]
    grid = (b, n_tiles, N_SLABS // K1_SLABS)

    def slab_out(width, dtype):
        return (jax.ShapeDtypeStruct((b, N_SLABS, rows, width), dtype),
                pl.BlockSpec((None, K1_SLABS, TILE_ROWS, width), lambda bi, n, g: (bi, g, n, 0)))

    outs = [slab_out(D_MODEL, F32), slab_out(2 * KV_WIDTH, F32)] + [slab_out(D_MODEL, BF16)] * 4
    return pl.pallas_call(
        _inproj_kernel,
        grid=grid,
        in_specs=[
            pl.BlockSpec((None, TILE_ROWS, K1_SLABS * D_MODEL), lambda bi, n, g: (bi, n, g)),
            pl.BlockSpec((1, D_MODEL), lambda bi, n, g: (0, 0)),
            pl.BlockSpec(w_perm.shape, lambda bi, n, g: (0, 0)),
        ],
        out_specs=[o[1] for o in outs],
        out_shape=[o[0] for o in outs],
        compiler_params=pltpu.CompilerParams(
            dimension_semantics=("arbitrary", "arbitrary", "arbitrary"),
            vmem_limit_bytes=VMEM_LIMIT),
        name="inproj",
    )(x3, norm_g, w_perm)


def _attn_bias():
    slopes = jnp.exp2(-8.0 * (jnp.arange(N_Q_HEADS, dtype=F32) + 1.0) / N_Q_HEADS)
    out = []
    for d, nm, c in PATTERNS:
        a = np.arange(TILE_ROWS)
        jq = (a % c) * nm + a // c
        col = np.arange(2 * TILE_ROWS)
        part, rem = col // TILE_ROWS, col % TILE_ROWS
        jk = (rem % c) * nm + rem // c - (1 - part) * TILE_ROWS
        dist = jq[:, None] - jk[None, :]
        valid = (dist >= 0) & (dist <= WINDOW_STEPS)
        bias = -slopes[:, None, None] * jnp.asarray(dist * d, F32)[None]
        v0 = jnp.where(valid[None], bias, -jnp.inf)
        v1 = jnp.where((valid & (part[None, :] == 1))[None], bias, -jnp.inf)
        out.append(jnp.stack([v0, v1]).reshape(2, N_KV_HEADS, Q_PER_KV * TILE_ROWS, 2 * TILE_ROWS))
    return jnp.stack(out)


def _attn_kernel(q_ref, kvc_ref, kvp_ref, bias_ref, ag_ref, y_ref, ks0, ks1, ks2, o_scr, l_scr):
    n = pl.program_id(2)
    ksubs = (ks0, ks1, ks2)
    for p, (d, nm, c) in enumerate(PATTERNS):
        ksub = ksubs[p]

        def sub_body(rd, carry, p=p, d=d, nm=nm, c=c, ksub=ksub):
            for m in range(nm):
                slab = rd + d * m
                ksub[0, m] = kvp_ref[slab, TILE_ROWS - c:TILE_ROWS, :]
                ksub[pl.ds(1, nm), m] = kvc_ref[slab].reshape(nm, c, GROUP_KV)

            def blk_body(u, carry2):
                row0 = pl.multiple_of(u * c, c)
                pieces = []
                for h in range(Q_PER_KV):
                    for m in range(nm):
                        pieces.append(q_ref[rd + d * m, pl.ds(row0, c), h * HEAD_DIM:(h + 1) * HEAD_DIM])
                qs = jnp.concatenate(pieces, axis=0).astype(BF16)
                kv = ksub[pl.ds(u, 2)].reshape(2 * TILE_ROWS, GROUP_KV)
                kb = kv[:, :HEAD_DIM].astype(BF16)
                vb = kv[:, HEAD_DIM:].astype(BF16)
                s = lax.dot_general(qs, kb, (((1,), (1,)), ((), ())), preferred_element_type=F32)
                first = jnp.logical_and(n == 0, u == 0).astype(jnp.int32)
                s = s + bias_ref[p, first]
                mx = jnp.max(s, axis=-1, keepdims=True)
                pe = jnp.exp(s - mx)
                l = jnp.sum(pe, axis=-1, keepdims=True)
                acc = jnp.dot(pe.astype(BF16), vb, preferred_element_type=F32)
                o = acc / l
                lse = mx + jnp.log(l)
                o4 = jnp.concatenate(
                    [o[h * TILE_ROWS:(h + 1) * TILE_ROWS] for h in range(Q_PER_KV)], axis=-1)
                l4 = jnp.concatenate(
                    [jnp.broadcast_to(lse[h * TILE_ROWS:(h + 1) * TILE_ROWS], (TILE_ROWS, HEAD_DIM))
                     for h in range(Q_PER_KV)], axis=-1)
                for m in range(nm):
                    o_scr[p, rd + d * m, pl.ds(row0, c), :] = o4[m * c:(m + 1) * c]
                    l_scr[p, rd + d * m, pl.ds(row0, c), :] = l4[m * c:(m + 1) * c]
                return carry2

            lax.fori_loop(0, nm, blk_body, 0)
            return carry

        lax.fori_loop(0, d, sub_body, 0)

    def comb_body(slab, carry):
        l0, l1, l2 = l_scr[0, slab], l_scr[1, slab], l_scr[2, slab]
        mx = jnp.maximum(jnp.maximum(l0, l1), l2)
        e0, e1, e2 = jnp.exp(l0 - mx), jnp.exp(l1 - mx), jnp.exp(l2 - mx)
        den = e0 + e1 + e2
        o = (e0 * o_scr[0, slab] + e1 * o_scr[1, slab] + e2 * o_scr[2, slab]) / den
        gate = ag_ref[slab].astype(F32)
        y_ref[slab] = (o * _silu(gate)).astype(y_ref.dtype)
        return carry

    lax.fori_loop(0, N_SLABS, comb_body, 0)


def _attention(q, kv, ag, bias):
    b, _, rows, _ = q.shape
    n_tiles = rows // TILE_ROWS
    grid = (N_KV_HEADS, b, n_tiles)
    scratch = [pltpu.VMEM((nm + 1, nm, c, GROUP_KV), F32) for _, nm, c in PATTERNS]
    scratch += [pltpu.VMEM((3, N_SLABS, TILE_ROWS, GROUP_Q), F32)] * 2
    return pl.pallas_call(
        _attn_kernel,
        grid=grid,
        in_specs=[
            pl.BlockSpec((None, N_SLABS, TILE_ROWS, GROUP_Q), lambda g, bi, n: (bi, 0, n, g)),
            pl.BlockSpec((None, N_SLABS, TILE_ROWS, GROUP_KV), lambda g, bi, n: (bi, 0, n, g)),
            pl.BlockSpec((None, N_SLABS, TILE_ROWS, GROUP_KV),
                         lambda g, bi, n: (bi, 0, jnp.maximum(n - 1, 0), g)),
            pl.BlockSpec((3, 2, None, Q_PER_KV * TILE_ROWS, 2 * TILE_ROWS),
                         lambda g, bi, n: (0, 0, g, 0, 0)),
            pl.BlockSpec((None, N_SLABS, TILE_ROWS, GROUP_Q), lambda g, bi, n: (bi, 0, n, g)),
        ],
        out_specs=pl.BlockSpec((None, N_SLABS, TILE_ROWS, GROUP_Q), lambda g, bi, n: (bi, 0, n, g)),
        out_shape=jax.ShapeDtypeStruct((b, N_SLABS, rows, D_MODEL), BF16),
        scratch_shapes=scratch,
        compiler_params=pltpu.CompilerParams(
            dimension_semantics=("arbitrary", "arbitrary", "arbitrary"),
            vmem_limit_bytes=VMEM_LIMIT),
        name="attention",
    )(q, kv, kv, bias, ag)


def _conv_kernel(cv_ref, cu_ref, cvp_ref, cup_ref, cg_ref, w_ref, b_ref, lg_ref, lb_ref, y_ref, hs_ref):
    t = pl.program_id(1)

    def prep(rr, carry):
        hc = cv_ref[rr].astype(F32) * _sigmoid(cu_ref[rr].astype(F32))
        hp = cvp_ref[rr].astype(F32) * _sigmoid(cup_ref[rr].astype(F32))
        hp = jnp.where(t == 0, 0.0, hp)
        full = jnp.concatenate([hp, hc], axis=0)
        hs_ref[0, rr] = hc
        hs_ref[1, rr] = pltpu.roll(full, 1, 0)[PREV_ROWS:]
        hs_ref[2, rr] = pltpu.roll(full, 2, 0)[PREV_ROWS:]
        return carry

    lax.fori_loop(0, N_SLABS, prep, 0)

    def out_body(r, carry):
        def tile_body(ti, carry2):
            row0 = pl.multiple_of(ti * CONV_TILE, CONV_TILE)
            acc = jnp.broadcast_to(b_ref[...], (CONV_TILE, D_MODEL))
            for s in range(CONV_K):
                rr = jnp.bitwise_and(r - s, N_SLABS - 1)
                down = jnp.right_shift(s - r + N_SLABS - 1, 4)
                acc = acc + w_ref[CONV_K - 1 - s:CONV_K - s, :] * hs_ref[down, rr, pl.ds(row0, CONV_TILE), :]
            mu = jnp.mean(acc, axis=-1, keepdims=True)
            cen = acc - mu
            var = jnp.mean(cen * cen, axis=-1, keepdims=True)
            z = cen * lax.rsqrt(var + LN_EPS) * lg_ref[...] + lb_ref[...]
            gate = cg_ref[r, pl.ds(row0, CONV_TILE), :].astype(F32)
            y_ref[r, pl.ds(row0, CONV_TILE), :] = (_silu(z) * _silu(gate)).astype(y_ref.dtype)
            return carry2

        lax.fori_loop(0, CONV_ROWS // CONV_TILE, tile_body, 0)
        return carry

    lax.fori_loop(0, N_SLABS, out_body, 0)


def _conv(cv, cu, cg, conv_w, conv_b, ln_g, ln_b):
    b, _, rows, _ = cv.shape
    grid = (b, rows // CONV_ROWS)
    cur = pl.BlockSpec((None, N_SLABS, CONV_ROWS, D_MODEL), lambda bi, t: (bi, 0, t, 0))
    prev = pl.BlockSpec((None, N_SLABS, PREV_ROWS, D_MODEL),
                        lambda bi, t: (bi, 0, jnp.maximum(t * (CONV_ROWS // PREV_ROWS) - 1, 0), 0))
    vec = pl.BlockSpec((1, D_MODEL), lambda bi, t: (0, 0))
    return pl.pallas_call(
        _conv_kernel,
        grid=grid,
        in_specs=[cur, cur, prev, prev, cur,
                  pl.BlockSpec((CONV_K, D_MODEL), lambda bi, t: (0, 0)), vec, vec, vec],
        out_specs=cur,
        out_shape=jax.ShapeDtypeStruct(cv.shape, BF16),
        scratch_shapes=[pltpu.VMEM((3, N_SLABS, CONV_ROWS, D_MODEL), F32)],
        compiler_params=pltpu.CompilerParams(
            dimension_semantics=("arbitrary", "arbitrary"),
            vmem_limit_bytes=VMEM_LIMIT),
        name="conv",
    )(cv, cu, cv, cu, cg, conv_w, conv_b, ln_g, ln_b)


def _outproj_kernel(ya_ref, yc_ref, x_ref, wa_ref, wc_ref, g_ref, o_ref):
    ya = ya_ref[...].reshape(K3_SLABS * TILE_ROWS, D_MODEL)
    yc = yc_ref[...].reshape(K3_SLABS * TILE_ROWS, D_MODEL)
    delta = (jnp.dot(ya, wa_ref[...], preferred_element_type=F32)
             + jnp.dot(yc, wc_ref[...], preferred_element_type=F32))
    for s in range(K3_SLABS):
        xs = x_ref[:, s * D_MODEL:(s + 1) * D_MODEL] + delta[s * TILE_ROWS:(s + 1) * TILE_ROWS]
        ms = jnp.mean(xs * xs, axis=-1, keepdims=True)
        o_ref[:, s * D_MODEL:(s + 1) * D_MODEL] = xs * lax.rsqrt(ms + NORM_EPS) * g_ref[...]


def _outproj(ya, yc, x3, w_att, w_conv, final_g):
    b, rows, _ = x3.shape
    grid = (b, rows // TILE_ROWS, N_SLABS // K3_SLABS)
    slab = pl.BlockSpec((None, K3_SLABS, TILE_ROWS, D_MODEL), lambda bi, n, g: (bi, g, n, 0))
    nat = pl.BlockSpec((None, TILE_ROWS, K3_SLABS * D_MODEL), lambda bi, n, g: (bi, n, g))
    wspec = pl.BlockSpec((D_MODEL, D_MODEL), lambda bi, n, g: (0, 0))
    return pl.pallas_call(
        _outproj_kernel,
        grid=grid,
        in_specs=[slab, slab, nat, wspec, wspec, pl.BlockSpec((1, D_MODEL), lambda bi, n, g: (0, 0))],
        out_specs=nat,
        out_shape=jax.ShapeDtypeStruct(x3.shape, F32),
        compiler_params=pltpu.CompilerParams(
            dimension_semantics=("arbitrary", "arbitrary", "arbitrary"),
            vmem_limit_bytes=VMEM_LIMIT),
        name="outproj",
    )(ya, yc, x3, w_att, w_conv, final_g)


def _permute_in_weights(w):
    wq = w[:, :D_MODEL]
    wk = w[:, D_MODEL:D_MODEL + KV_WIDTH].reshape(D_MODEL, N_KV_HEADS, HEAD_DIM)
    wv = w[:, D_MODEL + KV_WIDTH:D_MODEL + 2 * KV_WIDTH].reshape(D_MODEL, N_KV_HEADS, HEAD_DIM)
    wkv = jnp.concatenate([wk, wv], axis=-1).reshape(D_MODEL, 2 * KV_WIDTH)
    return jnp.concatenate([wq, wkv, w[:, D_MODEL + 2 * KV_WIDTH:]], axis=1).astype(BF16)


def kernel(x, norm_g, w_in, conv_w, conv_b, conv_ln_g, conv_ln_b, w_out, final_norm_g):
    b, seq, dm = x.shape
    assert dm == D_MODEL and seq % (N_SLABS * TILE_ROWS) == 0
    assert norm_g.shape[0] == 1, "single layer"
    x3 = x.reshape(b, seq // N_SLABS, N_SLABS * D_MODEL)
    w_perm = _permute_in_weights(w_in[0])
    q, kv, ag, cv, cu, cg = _inproj(x3, norm_g[0][None], w_perm)
    y_att = _attention(q, kv, ag, _attn_bias())
    y_conv = _conv(cv, cu, cg, conv_w[0], conv_b[0][None], conv_ln_g[0][None], conv_ln_b[0][None])
    w_o = w_out[0].astype(BF16)
    out3 = _outproj(y_att, y_conv, x3, w_o[:D_MODEL], w_o[D_MODEL:], final_norm_g[None])
    return out3.reshape(b, seq, dm)
```

```python
import math

import jax
import jax.numpy as jnp
import numpy as np
from jax import lax
from jax.experimental import pallas as pl
from jax.experimental.pallas import tpu as pltpu

F32 = jnp.float32
BF16 = jnp.bfloat16

D_MODEL = 1024
HEAD_DIM = 64
N_Q_HEADS = 16
N_KV_HEADS = 4
Q_PER_KV = N_Q_HEADS // N_KV_HEADS
KV_WIDTH = N_KV_HEADS * HEAD_DIM
CONV_K = 31
WINDOW_STEPS = 128
NORM_EPS = 1e-6
LN_EPS = 1e-5
LOG2E = math.log2(math.e)
LN2 = math.log(2.0)

N_SLABS = 16
TILE_ROWS = 128
PATTERNS = ((1, 16, 8), (4, 4, 32), (16, 1, 128))
BLOCKS_PER_PATTERN = 16
GROUP_Q = Q_PER_KV * HEAD_DIM
GROUP_KV = 2 * HEAD_DIM
PAIR = 2 * HEAD_DIM
N_PAIRS = Q_PER_KV // 2

K1_SLABS = 4
K3_SLABS = 4
CONV_ROWS = 64
CONV_TILE = 16
TAP_TILE = 16
PREV_ROWS = 16
SUBLANES = 8
VMEM_LIMIT = 56 * 1024 * 1024


def _sigmoid(x):
    return 1.0 / (1.0 + jnp.exp(-x))


def _silu(x):
    return x * _sigmoid(x)


def _inproj_kernel(x_ref, g_ref, w_ref, q_ref, kv_ref, ag_ref, cv_ref, cu_ref, cg_ref):
    hs = []
    for s in range(K1_SLABS):
        xs = x_ref[:, s * D_MODEL:(s + 1) * D_MODEL]
        ms = jnp.mean(xs * xs, axis=-1, keepdims=True)
        hs.append((xs * lax.rsqrt(ms + NORM_EPS) * g_ref[...]).astype(BF16))
    h = jnp.concatenate(hs, axis=0)

    def emit(ref, c0, width, scale):
        for cc in range(0, width, 512):
            wc = min(512, width - cc)
            r = jnp.dot(h, w_ref[:, c0 + cc:c0 + cc + wc], preferred_element_type=F32)
            if scale is not None:
                r = r * scale
            for s in range(K1_SLABS):
                ref[s, :, cc:cc + wc] = r[s * TILE_ROWS:(s + 1) * TILE_ROWS].astype(ref.dtype)

    c = 0
    emit(q_ref, c, D_MODEL, HEAD_DIM ** -0.5 * LOG2E)
    c += D_MODEL
    emit(kv_ref, c, 2 * KV_WIDTH, None)
    c += 2 * KV_WIDTH
    for ref in (ag_ref, cv_ref, cu_ref, cg_ref):
        emit(ref, c, D_MODEL, None)
        c += D_MODEL


def _inproj(x3, norm_g, w_perm):
    b = x3.shape[0]
    n_tiles = x3.shape[1] // TILE_ROWS
    rows = x3.shape[1]
    grid = (b, n_tiles, N_SLABS // K1_SLABS)

    def slab_out(width, dtype):
        return (jax.ShapeDtypeStruct((b, N_SLABS, rows, width), dtype),
                pl.BlockSpec((None, K1_SLABS, TILE_ROWS, width), lambda bi, n, g: (bi, g, n, 0)))

    outs = [slab_out(D_MODEL, F32), slab_out(2 * KV_WIDTH, F32)] + [slab_out(D_MODEL, BF16)] * 4
    return pl.pallas_call(
        _inproj_kernel,
        grid=grid,
        in_specs=[
            pl.BlockSpec((None, TILE_ROWS, K1_SLABS * D_MODEL), lambda bi, n, g: (bi, n, g)),
            pl.BlockSpec((1, D_MODEL), lambda bi, n, g: (0, 0)),
            pl.BlockSpec(w_perm.shape, lambda bi, n, g: (0, 0)),
        ],
        out_specs=[o[1] for o in outs],
        out_shape=[o[0] for o in outs],
        compiler_params=pltpu.CompilerParams(
            dimension_semantics=("arbitrary", "arbitrary", "arbitrary"),
            vmem_limit_bytes=VMEM_LIMIT),
        name="inproj",
    )(x3, norm_g, w_perm)


def _attn_bias():
    slopes = jnp.exp2(-8.0 * (jnp.arange(N_Q_HEADS, dtype=F32) + 1.0) / N_Q_HEADS)
    out = []
    for d, nm, c in PATTERNS:
        a = np.arange(TILE_ROWS)
        jq = (a % c) * nm + a // c
        col = np.arange(2 * TILE_ROWS)
        part, rem = col // TILE_ROWS, col % TILE_ROWS
        jk = (rem % c) * nm + rem // c - (1 - part) * TILE_ROWS
        dist = jq[:, None] - jk[None, :]
        valid = (dist >= 0) & (dist <= WINDOW_STEPS)
        bias = -slopes[:, None, None] * jnp.asarray(dist * d, F32)[None] * LOG2E
        v0 = jnp.where(valid[None], bias, -jnp.inf)
        v1 = jnp.where((valid & (part[None, :] == 1))[None], bias, -jnp.inf)
        both = jnp.stack([v0, v1])
        both = both.reshape(2, N_KV_HEADS, N_PAIRS, 2, TILE_ROWS, 2 * TILE_ROWS)
        both = both.transpose(0, 1, 2, 4, 3, 5)
        out.append(both.reshape(2, N_KV_HEADS, N_PAIRS * TILE_ROWS, 4 * TILE_ROWS))
    return jnp.stack(out)


def _attn_kernel(q_ref, kvc_ref, kvp_ref, bias_ref, ag_ref, y_ref, kvr, o_scr, l_scr):
    first_tile = (pl.program_id(2) == 0).astype(jnp.int32)
    lo = lax.broadcasted_iota(jnp.int32, (1, PAIR), 1) < HEAD_DIM
    ones_top = jnp.broadcast_to(jnp.where(lo, 1.0, 0.0), (2 * TILE_ROWS, PAIR))
    ones_rhs = jnp.concatenate([ones_top, 1.0 - ones_top], axis=0).astype(BF16)

    for s in range(N_SLABS):
        kvr[s, :TILE_ROWS] = pltpu.roll(kvp_ref[s], HEAD_DIM, 1)
        kvr[s, TILE_ROWS:] = pltpu.roll(kvc_ref[s], HEAD_DIM, 1)

    def block(p, rd, u):
        d, nm, c = PATTERNS[p]
        slabs = [rd + d * m for m in range(nm)]
        r0 = u * c
        qs = jnp.concatenate([q_ref[sl, r0:r0 + c, pr * PAIR:(pr + 1) * PAIR]
                              for pr in range(N_PAIRS) for sl in slabs], axis=0).astype(BF16)
        if u == 0:
            prev_a = [kvp_ref[sl, TILE_ROWS - c:TILE_ROWS, :] for sl in slabs]
        else:
            prev_a = [kvc_ref[sl, r0 - c:r0, :] for sl in slabs]
        kv_a = jnp.concatenate(prev_a + [kvc_ref[sl, r0:r0 + c, :] for sl in slabs], axis=0)
        kv_b = jnp.concatenate([kvr[sl, TILE_ROWS + r0 - c:TILE_ROWS + r0, :] for sl in slabs]
                               + [kvr[sl, TILE_ROWS + r0:TILE_ROWS + r0 + c, :] for sl in slabs],
                               axis=0)
        k_nt = jnp.concatenate([jnp.where(lo, kv_a, 0.0), jnp.where(lo, 0.0, kv_b)],
                               axis=0).astype(BF16)
        v_bd = jnp.concatenate([jnp.where(lo, kv_b, 0.0), jnp.where(lo, 0.0, kv_a)],
                               axis=0).astype(BF16)
        s = lax.dot_general(qs, k_nt, (((1,), (1,)), ((), ())), preferred_element_type=F32)
        s = s + (bias_ref[p, first_tile] if u == 0 else bias_ref[p, 0])
        sa, sb = s[:, :2 * TILE_ROWS], s[:, 2 * TILE_ROWS:]
        ma = jnp.max(sa, axis=-1, keepdims=True)
        mb = jnp.max(sb, axis=-1, keepdims=True)
        pe = jnp.concatenate([jnp.exp2(sa - ma), jnp.exp2(sb - mb)], axis=1).astype(BF16)
        rhs = jnp.concatenate([v_bd, ones_rhs], axis=1)
        r = jnp.dot(pe, rhs, preferred_element_type=F32)
        num, den = r[:, :PAIR], r[:, PAIR:]
        return num / den, jnp.where(lo, ma, mb) * LN2 + jnp.log(den)

    for p in range(len(PATTERNS) - 1):
        d, nm, c = PATTERNS[p]
        for blk in range(BLOCKS_PER_PATTERN):
            rd, u = blk // nm, blk % nm
            o, lse = block(p, rd, u)
            for pr in range(N_PAIRS):
                for m in range(nm):
                    rows = slice(pr * TILE_ROWS + m * c, pr * TILE_ROWS + (m + 1) * c)
                    lanes = slice(pr * PAIR, (pr + 1) * PAIR)
                    o_scr[p, rd + d * m, u * c:(u + 1) * c, lanes] = o[rows]
                    l_scr[p, rd + d * m, u * c:(u + 1) * c, lanes] = lse[rows]

    p_last = len(PATTERNS) - 1
    assert PATTERNS[p_last][1:] == (1, TILE_ROWS)
    for slab in range(N_SLABS):
        o, lse = block(p_last, slab, 0)
        o2 = jnp.concatenate([o[:TILE_ROWS], o[TILE_ROWS:]], axis=1)
        l2 = jnp.concatenate([lse[:TILE_ROWS], lse[TILE_ROWS:]], axis=1)
        l0, l1 = l_scr[0, slab], l_scr[1, slab]
        mx = jnp.maximum(jnp.maximum(l0, l1), l2)
        e0, e1, e2 = jnp.exp(l0 - mx), jnp.exp(l1 - mx), jnp.exp(l2 - mx)
        comb = (e0 * o_scr[0, slab] + e1 * o_scr[1, slab] + e2 * o2) / (e0 + e1 + e2)
        gate = ag_ref[slab].astype(F32)
        y_ref[slab] = (comb * _silu(gate)).astype(y_ref.dtype)


def _attention(q, kv, ag, bias):
    b, _, rows, _ = q.shape
    n_tiles = rows // TILE_ROWS
    grid = (N_KV_HEADS, b, n_tiles)
    scratch = [pltpu.VMEM((N_SLABS, 2 * TILE_ROWS, GROUP_KV), F32),
               pltpu.VMEM((len(PATTERNS) - 1, N_SLABS, TILE_ROWS, GROUP_Q), F32),
               pltpu.VMEM((len(PATTERNS) - 1, N_SLABS, TILE_ROWS, GROUP_Q), F32)]
    return pl.pallas_call(
        _attn_kernel,
        grid=grid,
        in_specs=[
            pl.BlockSpec((None, N_SLABS, TILE_ROWS, GROUP_Q), lambda g, bi, n: (bi, 0, n, g)),
            pl.BlockSpec((None, N_SLABS, TILE_ROWS, GROUP_KV), lambda g, bi, n: (bi, 0, n, g)),
            pl.BlockSpec((None, N_SLABS, TILE_ROWS, GROUP_KV),
                         lambda g, bi, n: (bi, 0, jnp.maximum(n - 1, 0), g)),
            pl.BlockSpec((3, 2, None, N_PAIRS * TILE_ROWS, 4 * TILE_ROWS),
                         lambda g, bi, n: (0, 0, g, 0, 0)),
            pl.BlockSpec((None, N_SLABS, TILE_ROWS, GROUP_Q), lambda g, bi, n: (bi, 0, n, g)),
        ],
        out_specs=pl.BlockSpec((None, N_SLABS, TILE_ROWS, GROUP_Q), lambda g, bi, n: (bi, 0, n, g)),
        out_shape=jax.ShapeDtypeStruct((b, N_SLABS, rows, D_MODEL), BF16),
        scratch_shapes=scratch,
        compiler_params=pltpu.CompilerParams(
            dimension_semantics=("arbitrary", "arbitrary", "arbitrary"),
            vmem_limit_bytes=VMEM_LIMIT),
        name="attention",
    )(q, kv, kv, bias, ag)


def _conv_kernel(cv_ref, cu_ref, cvp_ref, cup_ref, cg_ref, w_ref, b_ref, lg_ref, lb_ref, y_ref,
                 hs_ref, cs_ref):
    t = pl.program_id(1)
    tiles = CONV_ROWS // CONV_TILE
    tile_shift = tiles.bit_length() - 1

    def prep(rr, carry):
        hc = cv_ref[rr].astype(F32) * _sigmoid(cu_ref[rr].astype(F32))
        hp = cvp_ref[rr].astype(F32) * _sigmoid(cup_ref[rr].astype(F32))
        hp = jnp.where(t == 0, 0.0, hp)
        full = jnp.concatenate([hp, hc], axis=0)
        hs_ref[0, rr] = hc
        hs_ref[1, rr] = pltpu.roll(full, 1, 0)[PREV_ROWS:]
        hs_ref[2, rr] = pltpu.roll(full, 2, 0)[PREV_ROWS:]
        return carry

    lax.fori_loop(0, N_SLABS, prep, 0, unroll=2)

    tap_tiles = CONV_ROWS // TAP_TILE
    tap_shift = tap_tiles.bit_length() - 1

    def tap_body(it, carry):
        r = jnp.right_shift(it, tap_shift)
        row0 = pl.multiple_of(jnp.bitwise_and(it, tap_tiles - 1) * TAP_TILE, TAP_TILE)
        groups = TAP_TILE // SUBLANES
        accs = [jnp.broadcast_to(b_ref[...], (SUBLANES, D_MODEL))] * groups
        for s in range(CONV_K):
            rr = jnp.bitwise_and(r - s, N_SLABS - 1)
            down = jnp.right_shift(s - r + N_SLABS - 1, 4)
            w8 = w_ref[CONV_K - 1 - s]
            hblk = hs_ref[down, rr, pl.ds(row0, TAP_TILE), :]
            accs = [a + w8 * hblk[g * SUBLANES:(g + 1) * SUBLANES] for g, a in enumerate(accs)]
        for g, a in enumerate(accs):
            cs_ref[r, pl.ds(row0 + g * SUBLANES, SUBLANES), :] = a
        return carry

    lax.fori_loop(0, N_SLABS * tap_tiles, tap_body, 0)

    def ln_body(it, carry):
        r = jnp.right_shift(it, tile_shift)
        row0 = pl.multiple_of(jnp.bitwise_and(it, tiles - 1) * CONV_TILE, CONV_TILE)
        acc = cs_ref[r, pl.ds(row0, CONV_TILE), :]
        mu = jnp.mean(acc, axis=-1, keepdims=True)
        cen = acc - mu
        var = jnp.mean(cen * cen, axis=-1, keepdims=True)
        z = cen * lax.rsqrt(var + LN_EPS) * lg_ref[...] + lb_ref[...]
        gate = cg_ref[r, pl.ds(row0, CONV_TILE), :].astype(F32)
        y_ref[r, pl.ds(row0, CONV_TILE), :] = (_silu(z) * _silu(gate)).astype(y_ref.dtype)
        return carry

    lax.fori_loop(0, N_SLABS * tiles, ln_body, 0, unroll=4)


def _conv(cv, cu, cg, conv_w, conv_b, ln_g, ln_b):
    b, _, rows, _ = cv.shape
    grid = (b, rows // CONV_ROWS)
    cur = pl.BlockSpec((None, N_SLABS, CONV_ROWS, D_MODEL), lambda bi, t: (bi, 0, t, 0))
    prev = pl.BlockSpec((None, N_SLABS, PREV_ROWS, D_MODEL),
                        lambda bi, t: (bi, 0, jnp.maximum(t * (CONV_ROWS // PREV_ROWS) - 1, 0), 0))
    vec = pl.BlockSpec((1, D_MODEL), lambda bi, t: (0, 0))
    w_bcast = jnp.broadcast_to(conv_w[:, None, :], (CONV_K, SUBLANES, D_MODEL))
    return pl.pallas_call(
        _conv_kernel,
        grid=grid,
        in_specs=[cur, cur, prev, prev, cur,
                  pl.BlockSpec((CONV_K, SUBLANES, D_MODEL), lambda bi, t: (0, 0, 0)), vec, vec, vec],
        out_specs=cur,
        out_shape=jax.ShapeDtypeStruct(cv.shape, BF16),
        scratch_shapes=[pltpu.VMEM((3, N_SLABS, CONV_ROWS, D_MODEL), F32),
                        pltpu.VMEM((N_SLABS, CONV_ROWS, D_MODEL), F32)],
        compiler_params=pltpu.CompilerParams(
            dimension_semantics=("arbitrary", "arbitrary"),
            vmem_limit_bytes=VMEM_LIMIT),
        name="conv",
    )(cv, cu, cv, cu, cg, w_bcast, conv_b, ln_g, ln_b)


def _outproj_kernel(ya_ref, yc_ref, x_ref, wa_ref, wc_ref, g_ref, o_ref):
    ya = ya_ref[...].reshape(K3_SLABS * TILE_ROWS, D_MODEL)
    yc = yc_ref[...].reshape(K3_SLABS * TILE_ROWS, D_MODEL)
    delta = (jnp.dot(ya, wa_ref[...], preferred_element_type=F32)
             + jnp.dot(yc, wc_ref[...], preferred_element_type=F32))
    for s in range(K3_SLABS):
        xs = x_ref[:, s * D_MODEL:(s + 1) * D_MODEL] + delta[s * TILE_ROWS:(s + 1) * TILE_ROWS]
        ms = jnp.mean(xs * xs, axis=-1, keepdims=True)
        o_ref[:, s * D_MODEL:(s + 1) * D_MODEL] = xs * lax.rsqrt(ms + NORM_EPS) * g_ref[...]


def _outproj(ya, yc, x3, w_att, w_conv, final_g):
    b, rows, _ = x3.shape
    grid = (b, rows // TILE_ROWS, N_SLABS // K3_SLABS)
    slab = pl.BlockSpec((None, K3_SLABS, TILE_ROWS, D_MODEL), lambda bi, n, g: (bi, g, n, 0))
    nat = pl.BlockSpec((None, TILE_ROWS, K3_SLABS * D_MODEL), lambda bi, n, g: (bi, n, g))
    wspec = pl.BlockSpec((D_MODEL, D_MODEL), lambda bi, n, g: (0, 0))
    return pl.pallas_call(
        _outproj_kernel,
        grid=grid,
        in_specs=[slab, slab, nat, wspec, wspec, pl.BlockSpec((1, D_MODEL), lambda bi, n, g: (0, 0))],
        out_specs=nat,
        out_shape=jax.ShapeDtypeStruct(x3.shape, F32),
        compiler_params=pltpu.CompilerParams(
            dimension_semantics=("arbitrary", "arbitrary", "arbitrary"),
            vmem_limit_bytes=VMEM_LIMIT),
        name="outproj",
    )(ya, yc, x3, w_att, w_conv, final_g)


def _permute_in_weights(w):
    wq = w[:, :D_MODEL]
    wk = w[:, D_MODEL:D_MODEL + KV_WIDTH].reshape(D_MODEL, N_KV_HEADS, HEAD_DIM)
    wv = w[:, D_MODEL + KV_WIDTH:D_MODEL + 2 * KV_WIDTH].reshape(D_MODEL, N_KV_HEADS, HEAD_DIM)
    wkv = jnp.concatenate([wk, wv], axis=-1).reshape(D_MODEL, 2 * KV_WIDTH)
    return jnp.concatenate([wq, wkv, w[:, D_MODEL + 2 * KV_WIDTH:]], axis=1).astype(BF16)


def kernel(x, norm_g, w_in, conv_w, conv_b, conv_ln_g, conv_ln_b, w_out, final_norm_g):
    b, seq, dm = x.shape
    assert dm == D_MODEL and seq % (N_SLABS * TILE_ROWS) == 0
    assert norm_g.shape[0] == 1, "single layer"
    x3 = x.reshape(b, seq // N_SLABS, N_SLABS * D_MODEL)
    w_perm = _permute_in_weights(w_in[0])
    q, kv, ag, cv, cu, cg = _inproj(x3, norm_g[0][None], w_perm)
    y_att = _attention(q, kv, ag, _attn_bias())
    y_conv = _conv(cv, cu, cg, conv_w[0], conv_b[0][None], conv_ln_g[0][None], conv_ln_b[0][None])
    w_o = w_out[0].astype(BF16)
    out3 = _outproj(y_att, y_conv, x3, w_o[:D_MODEL], w_o[D_MODEL:], final_norm_g[None])
    return out3.reshape(b, seq, dm)
```

```python
import math

import jax
import jax.numpy as jnp
import numpy as np
from jax import lax
from jax.experimental import pallas as pl
from jax.experimental.pallas import tpu as pltpu

F32 = jnp.float32
BF16 = jnp.bfloat16

D_MODEL = 1024
HEAD_DIM = 64
N_Q_HEADS = 16
N_KV_HEADS = 4
Q_PER_KV = N_Q_HEADS // N_KV_HEADS
KV_WIDTH = N_KV_HEADS * HEAD_DIM
CONV_K = 31
WINDOW_STEPS = 128
NORM_EPS = 1e-6
LN_EPS = 1e-5
LOG2E = math.log2(math.e)
LN2 = math.log(2.0)

LANES = 128
SUBLANES = 8
N_SLABS = 16
TILE_ROWS = 128
PATTERNS = ((1, 16, 8), (4, 4, 32), (16, 1, 128))
BLOCKS_PER_PATTERN = 16
GROUP_Q = Q_PER_KV * HEAD_DIM
GROUP_KV = 2 * HEAD_DIM
PAIR = 2 * HEAD_DIM
N_PAIRS = Q_PER_KV // 2
COL_BLOCKS = D_MODEL // LANES

IN_ROWS = 32
CONV_ROWS = 64
CONV_TILE = 16
TAP_TILE = 16
PREV_ROWS = 16
OUT_ROWS = 64
VMEM_LIMIT = 56 * 1024 * 1024


def _sigmoid(x):
    return 1.0 / (1.0 + jnp.exp(-x))


def _silu(x):
    return x * _sigmoid(x)


def _slab_rows(x_refs, r, rows):
    return jnp.concatenate([xr[pl.ds(r, rows, stride=N_SLABS), :] for xr in x_refs], axis=-1)


def _inproj_kernel(*refs):
    x_refs = refs[:COL_BLOCKS]
    ng_ref, w_ref = refs[COL_BLOCKS:COL_BLOCKS + 2]
    q_ref, kv_ref, ag_ref, cv_ref, cu_ref, cg_ref = refs[COL_BLOCKS + 2:]
    hs = []
    for r in range(N_SLABS):
        xs = _slab_rows(x_refs, r, IN_ROWS)
        ms = jnp.mean(xs * xs, axis=-1, keepdims=True)
        hs.append((xs * lax.rsqrt(ms + NORM_EPS) * ng_ref[...]).astype(BF16))
    h = jnp.concatenate(hs, axis=0)

    def emit(ref, c0, width, scale):
        for cc in range(0, width, 512):
            wc = min(512, width - cc)
            res = jnp.dot(h, w_ref[:, c0 + cc:c0 + cc + wc], preferred_element_type=F32)
            if scale is not None:
                res = res * scale
            for r in range(N_SLABS):
                ref[r, :, cc:cc + wc] = res[r * IN_ROWS:(r + 1) * IN_ROWS].astype(ref.dtype)

    c = 0
    emit(q_ref, c, D_MODEL, HEAD_DIM ** -0.5 * LOG2E)
    c += D_MODEL
    emit(kv_ref, c, 2 * KV_WIDTH, None)
    c += 2 * KV_WIDTH
    for ref in (ag_ref, cv_ref, cu_ref, cg_ref):
        emit(ref, c, D_MODEL, None)
        c += D_MODEL


def _inproj(x, norm_g, w_perm):
    b, seq, _ = x.shape
    rows = seq // N_SLABS
    grid = (b, rows // IN_ROWS)

    def slab_out(width, dtype):
        return (jax.ShapeDtypeStruct((b, N_SLABS, rows, width), dtype),
                pl.BlockSpec((None, N_SLABS, IN_ROWS, width), lambda bi, j: (bi, 0, j, 0)))

    def x_spec(cb):
        return pl.BlockSpec((None, IN_ROWS * N_SLABS, LANES), lambda bi, j: (bi, j, cb))

    outs = [slab_out(D_MODEL, F32), slab_out(2 * KV_WIDTH, F32)] + [slab_out(D_MODEL, BF16)] * 4
    return pl.pallas_call(
        _inproj_kernel,
        grid=grid,
        in_specs=[x_spec(cb) for cb in range(COL_BLOCKS)] + [
            pl.BlockSpec((1, D_MODEL), lambda bi, j: (0, 0)),
            pl.BlockSpec(w_perm.shape, lambda bi, j: (0, 0))],
        out_specs=[o[1] for o in outs],
        out_shape=[o[0] for o in outs],
        compiler_params=pltpu.CompilerParams(
            dimension_semantics=("arbitrary", "arbitrary"),
            vmem_limit_bytes=VMEM_LIMIT),
        name="inproj",
    )(*([x] * COL_BLOCKS), norm_g, w_perm)


def _attn_bias():
    slopes = jnp.exp2(-8.0 * (jnp.arange(N_Q_HEADS, dtype=F32) + 1.0) / N_Q_HEADS)
    out = []
    for d, nm, c in PATTERNS:
        a = np.arange(TILE_ROWS)
        jq = (a % c) * nm + a // c
        col = np.arange(2 * TILE_ROWS)
        part, rem = col // TILE_ROWS, col % TILE_ROWS
        jk = (rem % c) * nm + rem // c - (1 - part) * TILE_ROWS
        dist = jq[:, None] - jk[None, :]
        valid = (dist >= 0) & (dist <= WINDOW_STEPS)
        bias = -slopes[:, None, None] * jnp.asarray(dist * d, F32)[None] * LOG2E
        v0 = jnp.where(valid[None], bias, -jnp.inf)
        v1 = jnp.where((valid & (part[None, :] == 1))[None], bias, -jnp.inf)
        both = jnp.stack([v0, v1])
        both = both.reshape(2, N_KV_HEADS, N_PAIRS, 2, TILE_ROWS, 2 * TILE_ROWS)
        both = both.transpose(0, 1, 2, 4, 3, 5)
        out.append(both.reshape(2, N_KV_HEADS, N_PAIRS * TILE_ROWS, 4 * TILE_ROWS))
    return jnp.stack(out)


def _attn_kernel(q_ref, kvc_ref, kvp_ref, bias_ref, ag_ref, y_ref, kvr, o_scr, l_scr):
    first_tile = (pl.program_id(2) == 0).astype(jnp.int32)
    lo = lax.broadcasted_iota(jnp.int32, (1, PAIR), 1) < HEAD_DIM
    ones_top = jnp.broadcast_to(jnp.where(lo, 1.0, 0.0), (2 * TILE_ROWS, PAIR))
    ones_rhs = jnp.concatenate([ones_top, 1.0 - ones_top], axis=0).astype(BF16)

    for s in range(N_SLABS):
        kvr[s, :TILE_ROWS] = pltpu.roll(kvp_ref[s], HEAD_DIM, 1)
        kvr[s, TILE_ROWS:] = pltpu.roll(kvc_ref[s], HEAD_DIM, 1)

    def block(p, rd, u):
        d, nm, c = PATTERNS[p]
        slabs = [rd + d * m for m in range(nm)]
        r0 = u * c
        qs = jnp.concatenate([q_ref[sl, r0:r0 + c, pr * PAIR:(pr + 1) * PAIR]
                              for pr in range(N_PAIRS) for sl in slabs], axis=0).astype(BF16)
        if u == 0:
            prev_a = [kvp_ref[sl, TILE_ROWS - c:TILE_ROWS, :] for sl in slabs]
        else:
            prev_a = [kvc_ref[sl, r0 - c:r0, :] for sl in slabs]
        kv_a = jnp.concatenate(prev_a + [kvc_ref[sl, r0:r0 + c, :] for sl in slabs], axis=0)
        kv_b = jnp.concatenate([kvr[sl, TILE_ROWS + r0 - c:TILE_ROWS + r0, :] for sl in slabs]
                               + [kvr[sl, TILE_ROWS + r0:TILE_ROWS + r0 + c, :] for sl in slabs],
                               axis=0)
        k_nt = jnp.concatenate([jnp.where(lo, kv_a, 0.0), jnp.where(lo, 0.0, kv_b)],
                               axis=0).astype(BF16)
        v_bd = jnp.concatenate([jnp.where(lo, kv_b, 0.0), jnp.where(lo, 0.0, kv_a)],
                               axis=0).astype(BF16)
        s = lax.dot_general(qs, k_nt, (((1,), (1,)), ((), ())), preferred_element_type=F32)
        s = s + (bias_ref[p, first_tile] if u == 0 else bias_ref[p, 0])
        sa, sb = s[:, :2 * TILE_ROWS], s[:, 2 * TILE_ROWS:]
        ma = jnp.max(sa, axis=-1, keepdims=True)
        mb = jnp.max(sb, axis=-1, keepdims=True)
        pe = jnp.concatenate([jnp.exp2(sa - ma), jnp.exp2(sb - mb)], axis=1).astype(BF16)
        rhs = jnp.concatenate([v_bd, ones_rhs], axis=1)
        r = jnp.dot(pe, rhs, preferred_element_type=F32)
        num, den = r[:, :PAIR], r[:, PAIR:]
        return num / den, jnp.where(lo, ma, mb) * LN2 + jnp.log(den)

    for p in range(len(PATTERNS) - 1):
        d, nm, c = PATTERNS[p]
        for blk in range(BLOCKS_PER_PATTERN):
            rd, u = blk // nm, blk % nm
            o, lse = block(p, rd, u)
            for pr in range(N_PAIRS):
                for m in range(nm):
                    rows = slice(pr * TILE_ROWS + m * c, pr * TILE_ROWS + (m + 1) * c)
                    lanes = slice(pr * PAIR, (pr + 1) * PAIR)
                    o_scr[p, rd + d * m, u * c:(u + 1) * c, lanes] = o[rows]
                    l_scr[p, rd + d * m, u * c:(u + 1) * c, lanes] = lse[rows]

    p_last = len(PATTERNS) - 1
    assert PATTERNS[p_last][1:] == (1, TILE_ROWS)
    for slab in range(N_SLABS):
        o, lse = block(p_last, slab, 0)
        o2 = jnp.concatenate([o[:TILE_ROWS], o[TILE_ROWS:]], axis=1)
        l2 = jnp.concatenate([lse[:TILE_ROWS], lse[TILE_ROWS:]], axis=1)
        l0, l1 = l_scr[0, slab], l_scr[1, slab]
        mx = jnp.maximum(jnp.maximum(l0, l1), l2)
        e0, e1, e2 = jnp.exp(l0 - mx), jnp.exp(l1 - mx), jnp.exp(l2 - mx)
        comb = (e0 * o_scr[0, slab] + e1 * o_scr[1, slab] + e2 * o2) / (e0 + e1 + e2)
        gate = ag_ref[slab].astype(F32)
        y_ref[slab] = (comb * _silu(gate)).astype(y_ref.dtype)


def _attention(q, kv, ag, bias):
    b, _, rows, _ = q.shape
    n_tiles = rows // TILE_ROWS
    grid = (N_KV_HEADS, b, n_tiles)
    scratch = [pltpu.VMEM((N_SLABS, 2 * TILE_ROWS, GROUP_KV), F32),
               pltpu.VMEM((len(PATTERNS) - 1, N_SLABS, TILE_ROWS, GROUP_Q), F32),
               pltpu.VMEM((len(PATTERNS) - 1, N_SLABS, TILE_ROWS, GROUP_Q), F32)]
    return pl.pallas_call(
        _attn_kernel,
        grid=grid,
        in_specs=[
            pl.BlockSpec((None, N_SLABS, TILE_ROWS, GROUP_Q), lambda g, bi, n: (bi, 0, n, g)),
            pl.BlockSpec((None, N_SLABS, TILE_ROWS, GROUP_KV), lambda g, bi, n: (bi, 0, n, g)),
            pl.BlockSpec((None, N_SLABS, TILE_ROWS, GROUP_KV),
                         lambda g, bi, n: (bi, 0, jnp.maximum(n - 1, 0), g)),
            pl.BlockSpec((3, 2, None, N_PAIRS * TILE_ROWS, 4 * TILE_ROWS),
                         lambda g, bi, n: (0, 0, g, 0, 0)),
            pl.BlockSpec((None, N_SLABS, TILE_ROWS, GROUP_Q), lambda g, bi, n: (bi, 0, n, g)),
        ],
        out_specs=pl.BlockSpec((None, N_SLABS, TILE_ROWS, GROUP_Q), lambda g, bi, n: (bi, 0, n, g)),
        out_shape=jax.ShapeDtypeStruct((b, N_SLABS, rows, D_MODEL), BF16),
        scratch_shapes=scratch,
        compiler_params=pltpu.CompilerParams(
            dimension_semantics=("arbitrary", "arbitrary", "arbitrary"),
            vmem_limit_bytes=VMEM_LIMIT),
        name="attention",
    )(q, kv, kv, bias, ag)


def _conv_kernel(cv_ref, cu_ref, cvp_ref, cup_ref, cg_ref, w_ref, b_ref, lg_ref, lb_ref, y_ref,
                 hs_ref, cs_ref):
    t = pl.program_id(1)
    tiles = CONV_ROWS // CONV_TILE
    tile_shift = tiles.bit_length() - 1

    def prep(rr, carry):
        hc = cv_ref[rr].astype(F32) * _sigmoid(cu_ref[rr].astype(F32))
        hp = cvp_ref[rr].astype(F32) * _sigmoid(cup_ref[rr].astype(F32))
        hp = jnp.where(t == 0, 0.0, hp)
        full = jnp.concatenate([hp, hc], axis=0)
        hs_ref[0, rr] = hc
        hs_ref[1, rr] = pltpu.roll(full, 1, 0)[PREV_ROWS:]
        hs_ref[2, rr] = pltpu.roll(full, 2, 0)[PREV_ROWS:]
        return carry

    lax.fori_loop(0, N_SLABS, prep, 0, unroll=2)

    tap_tiles = CONV_ROWS // TAP_TILE
    tap_shift = tap_tiles.bit_length() - 1

    def tap_body(it, carry):
        r = jnp.right_shift(it, tap_shift)
        row0 = pl.multiple_of(jnp.bitwise_and(it, tap_tiles - 1) * TAP_TILE, TAP_TILE)
        groups = TAP_TILE // SUBLANES
        accs = [jnp.broadcast_to(b_ref[...], (SUBLANES, D_MODEL))] * groups
        for s in range(CONV_K):
            rr = jnp.bitwise_and(r - s, N_SLABS - 1)
            down = jnp.right_shift(s - r + N_SLABS - 1, 4)
            w8 = w_ref[CONV_K - 1 - s]
            hblk = hs_ref[down, rr, pl.ds(row0, TAP_TILE), :]
            accs = [a + w8 * hblk[g * SUBLANES:(g + 1) * SUBLANES] for g, a in enumerate(accs)]
        for g, a in enumerate(accs):
            cs_ref[r, pl.ds(row0 + g * SUBLANES, SUBLANES), :] = a
        return carry

    lax.fori_loop(0, N_SLABS * tap_tiles, tap_body, 0)

    def ln_body(it, carry):
        r = jnp.right_shift(it, tile_shift)
        row0 = pl.multiple_of(jnp.bitwise_and(it, tiles - 1) * CONV_TILE, CONV_TILE)
        acc = cs_ref[r, pl.ds(row0, CONV_TILE), :]
        mu = jnp.mean(acc, axis=-1, keepdims=True)
        cen = acc - mu
        var = jnp.mean(cen * cen, axis=-1, keepdims=True)
        z = cen * lax.rsqrt(var + LN_EPS) * lg_ref[...] + lb_ref[...]
        gate = cg_ref[r, pl.ds(row0, CONV_TILE), :].astype(F32)
        y_ref[r, pl.ds(row0, CONV_TILE), :] = (_silu(z) * _silu(gate)).astype(y_ref.dtype)
        return carry

    lax.fori_loop(0, N_SLABS * tiles, ln_body, 0, unroll=4)


def _conv(cv, cu, cg, conv_w, conv_b, ln_g, ln_b):
    b, _, rows, _ = cv.shape
    grid = (b, rows // CONV_ROWS)
    cur = pl.BlockSpec((None, N_SLABS, CONV_ROWS, D_MODEL), lambda bi, t: (bi, 0, t, 0))
    prev = pl.BlockSpec((None, N_SLABS, PREV_ROWS, D_MODEL),
                        lambda bi, t: (bi, 0, jnp.maximum(t * (CONV_ROWS // PREV_ROWS) - 1, 0), 0))
    vec = pl.BlockSpec((1, D_MODEL), lambda bi, t: (0, 0))
    w_bcast = jnp.broadcast_to(conv_w[:, None, :], (CONV_K, SUBLANES, D_MODEL))
    return pl.pallas_call(
        _conv_kernel,
        grid=grid,
        in_specs=[cur, cur, prev, prev, cur,
                  pl.BlockSpec((CONV_K, SUBLANES, D_MODEL), lambda bi, t: (0, 0, 0)), vec, vec, vec],
        out_specs=cur,
        out_shape=jax.ShapeDtypeStruct(cv.shape, BF16),
        scratch_shapes=[pltpu.VMEM((3, N_SLABS, CONV_ROWS, D_MODEL), F32),
                        pltpu.VMEM((N_SLABS, CONV_ROWS, D_MODEL), F32)],
        compiler_params=pltpu.CompilerParams(
            dimension_semantics=("arbitrary", "arbitrary"),
            vmem_limit_bytes=VMEM_LIMIT),
        name="conv",
    )(cv, cu, cv, cu, cg, w_bcast, conv_b, ln_g, ln_b)


def _outproj_kernel(*refs):
    x_refs = refs[:COL_BLOCKS]
    ya_ref, yc_ref, wa_ref, wc_ref, g_ref, o_ref, stage = refs[COL_BLOCKS:]
    ya = ya_ref[...].reshape(N_SLABS * OUT_ROWS, D_MODEL)
    yc = yc_ref[...].reshape(N_SLABS * OUT_ROWS, D_MODEL)
    half = D_MODEL // 2
    deltas = [jnp.dot(ya, wa_ref[:, c:c + half], preferred_element_type=F32)
              + jnp.dot(yc, wc_ref[:, c:c + half], preferred_element_type=F32)
              for c in (0, half)]
    for r in range(N_SLABS):
        rows = slice(r * OUT_ROWS, (r + 1) * OUT_ROWS)
        xs = _slab_rows(x_refs, r, OUT_ROWS) + jnp.concatenate([dl[rows] for dl in deltas], axis=1)
        ms = jnp.mean(xs * xs, axis=-1, keepdims=True)
        res = xs * lax.rsqrt(ms + NORM_EPS) * g_ref[...]
        for cb in range(COL_BLOCKS):
            stage[cb, pl.ds(r, OUT_ROWS, stride=N_SLABS), :] = res[:, cb * LANES:(cb + 1) * LANES]
    for cb in range(COL_BLOCKS):
        o_ref[:, cb * LANES:(cb + 1) * LANES] = stage[cb]


def _outproj(ya, yc, x, w_att, w_conv, final_g):
    b, seq, _ = x.shape
    rows = seq // N_SLABS
    tok = OUT_ROWS * N_SLABS
    grid = (b, rows // OUT_ROWS)
    slab = pl.BlockSpec((None, N_SLABS, OUT_ROWS, D_MODEL), lambda bi, t: (bi, 0, t, 0))
    wspec = pl.BlockSpec((D_MODEL, D_MODEL), lambda bi, t: (0, 0))

    def x_spec(cb):
        return pl.BlockSpec((None, tok, LANES), lambda bi, t: (bi, t, cb))

    return pl.pallas_call(
        _outproj_kernel,
        grid=grid,
        in_specs=[x_spec(cb) for cb in range(COL_BLOCKS)] + [
            slab, slab, wspec, wspec, pl.BlockSpec((1, D_MODEL), lambda bi, t: (0, 0))],
        out_specs=pl.BlockSpec((None, tok, D_MODEL), lambda bi, t: (bi, t, 0)),
        out_shape=jax.ShapeDtypeStruct(x.shape, F32),
        scratch_shapes=[pltpu.VMEM((COL_BLOCKS, tok, LANES), F32)],
        compiler_params=pltpu.CompilerParams(
            dimension_semantics=("arbitrary", "arbitrary"),
            vmem_limit_bytes=VMEM_LIMIT),
        name="outproj",
    )(*([x] * COL_BLOCKS), ya, yc, w_att, w_conv, final_g)


def _permute_in_weights(w):
    wq = w[:, :D_MODEL]
    wk = w[:, D_MODEL:D_MODEL + KV_WIDTH].reshape(D_MODEL, N_KV_HEADS, HEAD_DIM)
    wv = w[:, D_MODEL + KV_WIDTH:D_MODEL + 2 * KV_WIDTH].reshape(D_MODEL, N_KV_HEADS, HEAD_DIM)
    wkv = jnp.concatenate([wk, wv], axis=-1).reshape(D_MODEL, 2 * KV_WIDTH)
    return jnp.concatenate([wq, wkv, w[:, D_MODEL + 2 * KV_WIDTH:]], axis=1).astype(BF16)


def kernel(x, norm_g, w_in, conv_w, conv_b, conv_ln_g, conv_ln_b, w_out, final_norm_g):
    b, seq, dm = x.shape
    assert dm == D_MODEL and seq % (N_SLABS * TILE_ROWS) == 0
    assert norm_g.shape[0] == 1, "single layer"
    w_perm = _permute_in_weights(w_in[0])
    q, kv, ag, cv, cu, cg = _inproj(x, norm_g[0][None], w_perm)
    y_att = _attention(q, kv, ag, _attn_bias())
    y_conv = _conv(cv, cu, cg, conv_w[0], conv_b[0][None], conv_ln_g[0][None], conv_ln_b[0][None])
    w_o = w_out[0].astype(BF16)
    return _outproj(y_att, y_conv, x, w_o[:D_MODEL], w_o[D_MODEL:], final_norm_g[None])
```

```python
import functools
import math

import jax
import jax.numpy as jnp
import numpy as np
from jax import lax
from jax.experimental import pallas as pl
from jax.experimental.pallas import tpu as pltpu

F32 = jnp.float32
BF16 = jnp.bfloat16

D_MODEL = 1024
HEAD_DIM = 64
N_Q_HEADS = 16
N_KV_HEADS = 4
Q_PER_KV = N_Q_HEADS // N_KV_HEADS
KV_WIDTH = N_KV_HEADS * HEAD_DIM
CONV_K = 31
WINDOW_STEPS = 128
NORM_EPS = 1e-6
LN_EPS = 1e-5
LOG2E = math.log2(math.e)
LN2 = math.log(2.0)

LANES = 128
SUBLANES = 8
N_SLABS = 16
TILE_ROWS = 128
PATTERNS = ((1, 16, 8), (4, 4, 32), (16, 1, 128))
BLOCKS_PER_PATTERN = 16
GROUP_Q = Q_PER_KV * HEAD_DIM
GROUP_KV = 2 * HEAD_DIM
PAIR = 2 * HEAD_DIM
N_PAIRS = Q_PER_KV // 2
COL_BLOCKS = D_MODEL // LANES

IN_ROWS = 32
CONV_ROWS = 64
CONV_TILE = 16
PREV_ROWS = 16
OUT_ROWS = 64
OUT_GROUP = 4
VMEM_LIMIT = 56 * 1024 * 1024


def _sigmoid(x):
    return 1.0 / (1.0 + jnp.exp(-x))


def _silu(x):
    return x * _sigmoid(x)


def _slab_rows(x_refs, r, rows):
    return jnp.concatenate([xr[pl.ds(r, rows, stride=N_SLABS), :] for xr in x_refs], axis=-1)


def _inproj_kernel(*refs):
    x_refs = refs[:COL_BLOCKS]
    ng_ref, w_ref = refs[COL_BLOCKS:COL_BLOCKS + 2]
    q_ref, kv_ref, ag_ref, cv_ref, cu_ref, cg_ref = refs[COL_BLOCKS + 2:]
    hs = []
    for r in range(N_SLABS):
        xs = _slab_rows(x_refs, r, IN_ROWS)
        ms = jnp.mean(xs * xs, axis=-1, keepdims=True)
        hs.append((xs * lax.rsqrt(ms + NORM_EPS) * ng_ref[...]).astype(BF16))
    h = jnp.concatenate(hs, axis=0)

    def emit(ref, c0, width, scale):
        for cc in range(0, width, 512):
            wc = min(512, width - cc)
            res = jnp.dot(h, w_ref[:, c0 + cc:c0 + cc + wc], preferred_element_type=F32)
            if scale is not None:
                res = res * scale
            for r in range(N_SLABS):
                ref[r, :, cc:cc + wc] = res[r * IN_ROWS:(r + 1) * IN_ROWS].astype(ref.dtype)

    c = 0
    emit(q_ref, c, D_MODEL, HEAD_DIM ** -0.5 * LOG2E)
    c += D_MODEL
    res_k = jnp.dot(h, w_ref[:, c:c + KV_WIDTH], preferred_element_type=F32)
    res_v = jnp.dot(h, w_ref[:, c + KV_WIDTH:c + 2 * KV_WIDTH], preferred_element_type=F32)
    heads = [res[:, g * HEAD_DIM:(g + 1) * HEAD_DIM] for g in range(N_KV_HEADS) for res in (res_k, res_v)]
    res_kv = jnp.concatenate(heads, axis=1)
    for r in range(N_SLABS):
        kv_ref[r] = res_kv[r * IN_ROWS:(r + 1) * IN_ROWS]
    c += 2 * KV_WIDTH
    for ref in (ag_ref, cv_ref, cu_ref, cg_ref):
        emit(ref, c, D_MODEL, None)
        c += D_MODEL


def _inproj(x, norm_g, w_perm):
    b, seq, _ = x.shape
    rows = seq // N_SLABS
    grid = (b, rows // IN_ROWS)

    def slab_out(width, dtype):
        return (jax.ShapeDtypeStruct((b, N_SLABS, rows, width), dtype),
                pl.BlockSpec((None, N_SLABS, IN_ROWS, width), lambda bi, j: (bi, 0, j, 0)))

    def x_spec(cb):
        return pl.BlockSpec((None, IN_ROWS * N_SLABS, LANES), lambda bi, j: (bi, j, cb))

    outs = [slab_out(D_MODEL, F32), slab_out(2 * KV_WIDTH, F32)] + [slab_out(D_MODEL, BF16)] * 4
    return pl.pallas_call(
        _inproj_kernel,
        grid=grid,
        in_specs=[x_spec(cb) for cb in range(COL_BLOCKS)] + [
            pl.BlockSpec((1, D_MODEL), lambda bi, j: (0, 0)),
            pl.BlockSpec(w_perm.shape, lambda bi, j: (0, 0))],
        out_specs=[o[1] for o in outs],
        out_shape=[o[0] for o in outs],
        compiler_params=pltpu.CompilerParams(
            dimension_semantics=("arbitrary", "arbitrary"),
            vmem_limit_bytes=VMEM_LIMIT),
        name="inproj",
    )(*([x] * COL_BLOCKS), norm_g, w_perm)


def _attn_bias():
    slopes = np.exp2(-8.0 * (np.arange(N_Q_HEADS, dtype=np.float32) + 1.0) / N_Q_HEADS)
    out = []
    for d, nm, c in PATTERNS:
        a = np.arange(TILE_ROWS)
        jq = (a % c) * nm + a // c
        col = np.arange(2 * TILE_ROWS)
        part, rem = col // TILE_ROWS, col % TILE_ROWS
        jk = (rem % c) * nm + rem // c - (1 - part) * TILE_ROWS
        dist = jq[:, None] - jk[None, :]
        valid = (dist >= 0) & (dist <= WINDOW_STEPS)
        bias = -slopes[:, None, None] * (dist * d).astype(np.float32)[None] * np.float32(LOG2E)
        v0 = np.where(valid[None], bias, -np.inf)
        v1 = np.where((valid & (part[None, :] == 1))[None], bias, -np.inf)
        both = np.stack([v0, v1])
        both = both.reshape(2, N_KV_HEADS, N_PAIRS, 2, TILE_ROWS, 2 * TILE_ROWS)
        both = both.transpose(0, 1, 2, 4, 3, 5)
        out.append(both.reshape(2, N_KV_HEADS, N_PAIRS * TILE_ROWS, 4 * TILE_ROWS))
    return jnp.asarray(np.stack(out), F32)


def _attn_kernel(q_ref, kvc_ref, kvp_ref, bias_ref, ag_ref, y_ref, kvr, o_scr, l_scr):
    first_tile = (pl.program_id(2) == 0).astype(jnp.int32)
    lo = lax.broadcasted_iota(jnp.int32, (1, PAIR), 1) < HEAD_DIM
    ones_top = jnp.broadcast_to(jnp.where(lo, 1.0, 0.0), (2 * TILE_ROWS, PAIR))
    ones_rhs = jnp.concatenate([ones_top, 1.0 - ones_top], axis=0).astype(BF16)

    for s in range(N_SLABS):
        kvr[s, :TILE_ROWS] = pltpu.roll(kvp_ref[s], HEAD_DIM, 1)
        kvr[s, TILE_ROWS:] = pltpu.roll(kvc_ref[s], HEAD_DIM, 1)

    def block(p, rd, u):
        d, nm, c = PATTERNS[p]
        slabs = [rd + d * m for m in range(nm)]
        r0 = u * c
        qs = jnp.concatenate([q_ref[sl, r0:r0 + c, pr * PAIR:(pr + 1) * PAIR]
                              for pr in range(N_PAIRS) for sl in slabs], axis=0).astype(BF16)
        if u == 0:
            prev_a = [kvp_ref[sl, TILE_ROWS - c:TILE_ROWS, :] for sl in slabs]
        else:
            prev_a = [kvc_ref[sl, r0 - c:r0, :] for sl in slabs]
        kv_a = jnp.concatenate(prev_a + [kvc_ref[sl, r0:r0 + c, :] for sl in slabs], axis=0)
        kv_b = jnp.concatenate([kvr[sl, TILE_ROWS + r0 - c:TILE_ROWS + r0, :] for sl in slabs]
                               + [kvr[sl, TILE_ROWS + r0:TILE_ROWS + r0 + c, :] for sl in slabs],
                               axis=0)
        k_nt = jnp.concatenate([jnp.where(lo, kv_a, 0.0), jnp.where(lo, 0.0, kv_b)],
                               axis=0).astype(BF16)
        v_bd = jnp.concatenate([jnp.where(lo, kv_b, 0.0), jnp.where(lo, 0.0, kv_a)],
                               axis=0).astype(BF16)
        s = lax.dot_general(qs, k_nt, (((1,), (1,)), ((), ())), preferred_element_type=F32)
        s = s + (bias_ref[p, first_tile] if u == 0 else bias_ref[p, 0])
        sa, sb = s[:, :2 * TILE_ROWS], s[:, 2 * TILE_ROWS:]
        ma = jnp.max(sa, axis=-1, keepdims=True)
        mb = jnp.max(sb, axis=-1, keepdims=True)
        pe = jnp.concatenate([jnp.exp2(sa - ma), jnp.exp2(sb - mb)], axis=1).astype(BF16)
        rhs = jnp.concatenate([v_bd, ones_rhs], axis=1)
        r = jnp.dot(pe, rhs, preferred_element_type=F32)
        num, den = r[:, :PAIR], r[:, PAIR:]
        return num * (1.0 / den), jnp.where(lo, ma, mb) * LN2 + jnp.log(den)

    for p in range(len(PATTERNS) - 1):
        d, nm, c = PATTERNS[p]
        for blk in range(BLOCKS_PER_PATTERN):
            rd, u = blk // nm, blk % nm
            o, lse = block(p, rd, u)
            for pr in range(N_PAIRS):
                for m in range(nm):
                    rows = slice(pr * TILE_ROWS + m * c, pr * TILE_ROWS + (m + 1) * c)
                    lanes = slice(pr * PAIR, (pr + 1) * PAIR)
                    o_scr[p, rd + d * m, u * c:(u + 1) * c, lanes] = o[rows]
                    l_scr[p, rd + d * m, u * c:(u + 1) * c, lanes] = lse[rows]

    p_last = len(PATTERNS) - 1
    assert PATTERNS[p_last][1:] == (1, TILE_ROWS)
    for slab in range(N_SLABS):
        o, lse = block(p_last, slab, 0)
        o2 = jnp.concatenate([o[:TILE_ROWS], o[TILE_ROWS:]], axis=1)
        l2 = jnp.concatenate([lse[:TILE_ROWS], lse[TILE_ROWS:]], axis=1)
        l0, l1 = l_scr[0, slab], l_scr[1, slab]
        mx = jnp.maximum(jnp.maximum(l0, l1), l2)
        e0, e1, e2 = jnp.exp(l0 - mx), jnp.exp(l1 - mx), jnp.exp(l2 - mx)
        comb = (e0 * o_scr[0, slab] + e1 * o_scr[1, slab] + e2 * o2) * (1.0 / (e0 + e1 + e2))
        gate = ag_ref[slab].astype(F32)
        y_ref[slab] = (comb * _silu(gate)).astype(y_ref.dtype)


def _attention(q, kv, ag, bias):
    b, _, rows, _ = q.shape
    n_tiles = rows // TILE_ROWS
    grid = (N_KV_HEADS, b, n_tiles)
    scratch = [pltpu.VMEM((N_SLABS, 2 * TILE_ROWS, GROUP_KV), F32),
               pltpu.VMEM((len(PATTERNS) - 1, N_SLABS, TILE_ROWS, GROUP_Q), F32),
               pltpu.VMEM((len(PATTERNS) - 1, N_SLABS, TILE_ROWS, GROUP_Q), F32)]
    return pl.pallas_call(
        _attn_kernel,
        grid=grid,
        in_specs=[
            pl.BlockSpec((None, N_SLABS, TILE_ROWS, GROUP_Q), lambda g, bi, n: (bi, 0, n, g)),
            pl.BlockSpec((None, N_SLABS, TILE_ROWS, GROUP_KV), lambda g, bi, n: (bi, 0, n, g)),
            pl.BlockSpec((None, N_SLABS, TILE_ROWS, GROUP_KV),
                         lambda g, bi, n: (bi, 0, jnp.maximum(n - 1, 0), g)),
            pl.BlockSpec((3, 2, None, N_PAIRS * TILE_ROWS, 4 * TILE_ROWS),
                         lambda g, bi, n: (0, 0, g, 0, 0)),
            pl.BlockSpec((None, N_SLABS, TILE_ROWS, GROUP_Q), lambda g, bi, n: (bi, 0, n, g)),
        ],
        out_specs=pl.BlockSpec((None, N_SLABS, TILE_ROWS, GROUP_Q), lambda g, bi, n: (bi, 0, n, g)),
        out_shape=jax.ShapeDtypeStruct((b, N_SLABS, rows, D_MODEL), BF16),
        scratch_shapes=scratch,
        compiler_params=pltpu.CompilerParams(
            dimension_semantics=("arbitrary", "arbitrary", "arbitrary"),
            vmem_limit_bytes=VMEM_LIMIT),
        name="attention",
    )(q, kv, kv, bias, ag)


def _conv_kernel(cv_ref, cu_ref, cvp_ref, cup_ref, cg_ref, w_ref, b_ref, lg_ref, lb_ref, y_ref,
                 hs_ref, cs_ref):
    t = pl.program_id(1)
    tiles = CONV_ROWS // CONV_TILE
    tile_shift = tiles.bit_length() - 1

    def prep(rr, carry):
        hc = cv_ref[rr].astype(F32) * _sigmoid(cu_ref[rr].astype(F32))
        hp = cvp_ref[rr].astype(F32) * _sigmoid(cup_ref[rr].astype(F32))
        hp = jnp.where(t == 0, 0.0, hp)
        full = jnp.concatenate([hp, hc], axis=0)
        hs_ref[0, rr] = hc
        hs_ref[1, rr] = pltpu.roll(full, 1, 0)[PREV_ROWS:]
        hs_ref[2, rr] = pltpu.roll(full, 2, 0)[PREV_ROWS:]
        return carry

    lax.fori_loop(0, N_SLABS, prep, 0, unroll=2)

    def tap_body(rg, carry, lanes):
        row0 = pl.multiple_of(rg * SUBLANES, SUBLANES)
        accs = [jnp.broadcast_to(b_ref[:, lanes], (SUBLANES, LANES))] * N_SLABS
        for s in range(CONV_K):
            w8 = w_ref[CONV_K - 1 - s, :, lanes]
            for r in range(N_SLABS):
                rr = (r - s) % N_SLABS
                down = (s - r + N_SLABS - 1) // N_SLABS
                accs[r] = accs[r] + w8 * hs_ref[down, rr, pl.ds(row0, SUBLANES), lanes]
        for r in range(N_SLABS):
            cs_ref[r, pl.ds(row0, SUBLANES), lanes] = accs[r]
        return carry

    for lt in range(COL_BLOCKS):
        lanes = slice(lt * LANES, (lt + 1) * LANES)
        lax.fori_loop(0, CONV_ROWS // SUBLANES, functools.partial(tap_body, lanes=lanes), 0)

    def ln_body(it, carry):
        r = jnp.right_shift(it, tile_shift)
        row0 = pl.multiple_of(jnp.bitwise_and(it, tiles - 1) * CONV_TILE, CONV_TILE)
        acc = cs_ref[r, pl.ds(row0, CONV_TILE), :]
        mu = jnp.mean(acc, axis=-1, keepdims=True)
        cen = acc - mu
        var = jnp.mean(cen * cen, axis=-1, keepdims=True)
        z = cen * lax.rsqrt(var + LN_EPS) * lg_ref[...] + lb_ref[...]
        gate = cg_ref[r, pl.ds(row0, CONV_TILE), :].astype(F32)
        y_ref[r, pl.ds(row0, CONV_TILE), :] = (_silu(z) * _silu(gate)).astype(y_ref.dtype)
        return carry

    lax.fori_loop(0, N_SLABS * tiles, ln_body, 0, unroll=4)


def _conv(cv, cu, cg, conv_w, conv_b, ln_g, ln_b):
    b, _, rows, _ = cv.shape
    grid = (b, rows // CONV_ROWS)
    cur = pl.BlockSpec((None, N_SLABS, CONV_ROWS, D_MODEL), lambda bi, t: (bi, 0, t, 0))
    prev = pl.BlockSpec((None, N_SLABS, PREV_ROWS, D_MODEL),
                        lambda bi, t: (bi, 0, jnp.maximum(t * (CONV_ROWS // PREV_ROWS) - 1, 0), 0))
    vec = pl.BlockSpec((1, D_MODEL), lambda bi, t: (0, 0))
    w_bcast = jnp.broadcast_to(conv_w[:, None, :], (CONV_K, SUBLANES, D_MODEL))
    return pl.pallas_call(
        _conv_kernel,
        grid=grid,
        in_specs=[cur, cur, prev, prev, cur,
                  pl.BlockSpec((CONV_K, SUBLANES, D_MODEL), lambda bi, t: (0, 0, 0)), vec, vec, vec],
        out_specs=cur,
        out_shape=jax.ShapeDtypeStruct(cv.shape, BF16),
        scratch_shapes=[pltpu.VMEM((3, N_SLABS, CONV_ROWS, D_MODEL), F32),
                        pltpu.VMEM((N_SLABS, CONV_ROWS, D_MODEL), F32)],
        compiler_params=pltpu.CompilerParams(
            dimension_semantics=("arbitrary", "arbitrary"),
            vmem_limit_bytes=VMEM_LIMIT),
        name="conv",
    )(cv, cu, cv, cu, cg, w_bcast, conv_b, ln_g, ln_b)


def _outproj_kernel(*refs):
    x_refs = refs[:COL_BLOCKS]
    ya_ref, yc_ref, wa_ref, wc_ref, g_ref, o_ref, stage = refs[COL_BLOCKS:]
    for g0 in range(0, N_SLABS, OUT_GROUP):
        ya = ya_ref[g0:g0 + OUT_GROUP].reshape(OUT_GROUP * OUT_ROWS, D_MODEL)
        yc = yc_ref[g0:g0 + OUT_GROUP].reshape(OUT_GROUP * OUT_ROWS, D_MODEL)
        delta = (jnp.dot(ya, wa_ref[...], preferred_element_type=F32)
                 + jnp.dot(yc, wc_ref[...], preferred_element_type=F32))
        for s in range(OUT_GROUP):
            r = g0 + s
            xs = _slab_rows(x_refs, r, OUT_ROWS) + delta[s * OUT_ROWS:(s + 1) * OUT_ROWS]
            ms = jnp.mean(xs * xs, axis=-1, keepdims=True)
            res = xs * lax.rsqrt(ms + NORM_EPS) * g_ref[...]
            for cb in range(COL_BLOCKS):
                stage[cb, pl.ds(r, OUT_ROWS, stride=N_SLABS), :] = res[:, cb * LANES:(cb + 1) * LANES]
    for cb in range(COL_BLOCKS):
        o_ref[:, cb * LANES:(cb + 1) * LANES] = stage[cb]


def _outproj(ya, yc, x, w_att, w_conv, final_g):
    b, seq, _ = x.shape
    rows = seq // N_SLABS
    tok = OUT_ROWS * N_SLABS
    grid = (b, rows // OUT_ROWS)
    slab = pl.BlockSpec((None, N_SLABS, OUT_ROWS, D_MODEL), lambda bi, t: (bi, 0, t, 0))
    wspec = pl.BlockSpec((D_MODEL, D_MODEL), lambda bi, t: (0, 0))

    def x_spec(cb):
        return pl.BlockSpec((None, tok, LANES), lambda bi, t: (bi, t, cb))

    return pl.pallas_call(
        _outproj_kernel,
        grid=grid,
        in_specs=[x_spec(cb) for cb in range(COL_BLOCKS)] + [
            slab, slab, wspec, wspec, pl.BlockSpec((1, D_MODEL), lambda bi, t: (0, 0))],
        out_specs=pl.BlockSpec((None, tok, D_MODEL), lambda bi, t: (bi, t, 0)),
        out_shape=jax.ShapeDtypeStruct(x.shape, F32),
        scratch_shapes=[pltpu.VMEM((COL_BLOCKS, tok, LANES), F32)],
        compiler_params=pltpu.CompilerParams(
            dimension_semantics=("arbitrary", "arbitrary"),
            vmem_limit_bytes=VMEM_LIMIT),
        name="outproj",
    )(*([x] * COL_BLOCKS), ya, yc, w_att, w_conv, final_g)


def kernel(x, norm_g, w_in, conv_w, conv_b, conv_ln_g, conv_ln_b, w_out, final_norm_g):
    b, seq, dm = x.shape
    assert dm == D_MODEL and seq % (N_SLABS * TILE_ROWS) == 0
    assert norm_g.shape[0] == 1, "single layer"
    q, kv, ag, cv, cu, cg = _inproj(x, norm_g[0][None], w_in[0].astype(BF16))
    y_att = _attention(q, kv, ag, _attn_bias())
    y_conv = _conv(cv, cu, cg, conv_w[0], conv_b[0][None], conv_ln_g[0][None], conv_ln_b[0][None])
    w_o = w_out[0].astype(BF16)
    return _outproj(y_att, y_conv, x, w_o[:D_MODEL], w_o[D_MODEL:], final_norm_g[None])
```

```python
import functools
import math

import jax
import jax.numpy as jnp
import numpy as np
from jax import lax
from jax.experimental import pallas as pl
from jax.experimental.pallas import tpu as pltpu

F32 = jnp.float32
BF16 = jnp.bfloat16

D_MODEL = 1024
HEAD_DIM = 64
N_Q_HEADS = 16
N_KV_HEADS = 4
Q_PER_KV = N_Q_HEADS // N_KV_HEADS
KV_WIDTH = N_KV_HEADS * HEAD_DIM
CONV_K = 31
WINDOW_STEPS = 128
NORM_EPS = 1e-6
LN_EPS = 1e-5
LOG2E = math.log2(math.e)

LANES = 128
SUBLANES = 8
N_SLABS = 16
TILE_ROWS = 128
PATTERNS = ((1, 16, 8), (4, 4, 32), (16, 1, 128))
BLOCKS_PER_PATTERN = 16
GROUP_Q = Q_PER_KV * HEAD_DIM
GROUP_KV = 2 * HEAD_DIM
PAIR = 2 * HEAD_DIM
N_PAIRS = Q_PER_KV // 2
COL_BLOCKS = D_MODEL // LANES

IN_ROWS = 32
CONV_ROWS = 64
CONV_TILE = 16
TAP_SLABS = 16
PREV_ROWS = 16
OUT_ROWS = 64
VMEM_LIMIT = 56 * 1024 * 1024


def _sigmoid(x):
    return 1.0 / (1.0 + jnp.exp(-x))


def _silu(x):
    return x * _sigmoid(x)


def _slab_rows(x_refs, r, rows):
    return jnp.concatenate([xr[pl.ds(r, rows, stride=N_SLABS), :] for xr in x_refs], axis=-1)


def _inproj_kernel(*refs):
    x_refs = refs[:COL_BLOCKS]
    ng_ref, w_ref = refs[COL_BLOCKS:COL_BLOCKS + 2]
    q_ref, kv_ref, ag_ref, cv_ref, cu_ref, cg_ref = refs[COL_BLOCKS + 2:]
    hs = []
    for r in range(N_SLABS):
        xs = _slab_rows(x_refs, r, IN_ROWS)
        ms = jnp.mean(xs * xs, axis=-1, keepdims=True)
        hs.append((xs * lax.rsqrt(ms + NORM_EPS) * ng_ref[...]).astype(BF16))
    h = jnp.concatenate(hs, axis=0)

    def emit(ref, c0, width, scale):
        for cc in range(0, width, 512):
            wc = min(512, width - cc)
            res = jnp.dot(h, w_ref[:, c0 + cc:c0 + cc + wc], preferred_element_type=F32)
            if scale is not None:
                res = res * scale
            for r in range(N_SLABS):
                ref[r, :, cc:cc + wc] = res[r * IN_ROWS:(r + 1) * IN_ROWS].astype(ref.dtype)

    c = 0
    emit(q_ref, c, D_MODEL, HEAD_DIM ** -0.5 * LOG2E)
    c += D_MODEL
    res_k = jnp.dot(h, w_ref[:, c:c + KV_WIDTH], preferred_element_type=F32)
    res_v = jnp.dot(h, w_ref[:, c + KV_WIDTH:c + 2 * KV_WIDTH], preferred_element_type=F32)
    heads = [res[:, g * HEAD_DIM:(g + 1) * HEAD_DIM] for g in range(N_KV_HEADS) for res in (res_k, res_v)]
    res_kv = jnp.concatenate(heads, axis=1)
    for r in range(N_SLABS):
        kv_ref[r] = res_kv[r * IN_ROWS:(r + 1) * IN_ROWS]
    c += 2 * KV_WIDTH
    for ref in (ag_ref, cv_ref, cu_ref, cg_ref):
        emit(ref, c, D_MODEL, None)
        c += D_MODEL


def _inproj(x, norm_g, w_perm):
    b, seq, _ = x.shape
    rows = seq // N_SLABS
    grid = (b, rows // IN_ROWS)

    def slab_out(width, dtype):
        return (jax.ShapeDtypeStruct((b, N_SLABS, rows, width), dtype),
                pl.BlockSpec((None, N_SLABS, IN_ROWS, width), lambda bi, j: (bi, 0, j, 0)))

    def x_spec(cb):
        return pl.BlockSpec((None, IN_ROWS * N_SLABS, LANES), lambda bi, j: (bi, j, cb))

    outs = [slab_out(D_MODEL, F32), slab_out(2 * KV_WIDTH, F32)] + [slab_out(D_MODEL, BF16)] * 4
    return pl.pallas_call(
        _inproj_kernel,
        grid=grid,
        in_specs=[x_spec(cb) for cb in range(COL_BLOCKS)] + [
            pl.BlockSpec((1, D_MODEL), lambda bi, j: (0, 0)),
            pl.BlockSpec(w_perm.shape, lambda bi, j: (0, 0))],
        out_specs=[o[1] for o in outs],
        out_shape=[o[0] for o in outs],
        compiler_params=pltpu.CompilerParams(
            dimension_semantics=("arbitrary", "arbitrary"),
            vmem_limit_bytes=VMEM_LIMIT),
        name="inproj",
    )(*([x] * COL_BLOCKS), norm_g, w_perm)


def _attn_bias():
    slopes = np.exp2(-8.0 * (np.arange(N_Q_HEADS, dtype=np.float32) + 1.0) / N_Q_HEADS)
    out = []
    for d, nm, c in PATTERNS:
        a = np.arange(TILE_ROWS)
        jq = (a % c) * nm + a // c
        col = np.arange(2 * TILE_ROWS)
        part, rem = col // TILE_ROWS, col % TILE_ROWS
        jk = (rem % c) * nm + rem // c - (1 - part) * TILE_ROWS
        dist = jq[:, None] - jk[None, :]
        valid = (dist >= 0) & (dist <= WINDOW_STEPS)
        bias = -slopes[:, None, None] * (dist * d).astype(np.float32)[None] * np.float32(LOG2E)
        v0 = np.where(valid[None], bias, -np.inf)
        v1 = np.where((valid & (part[None, :] == 1))[None], bias, -np.inf)
        both = np.stack([v0, v1])
        both = both.reshape(2, N_KV_HEADS, N_PAIRS, 2, TILE_ROWS, 2 * TILE_ROWS)
        both = both.transpose(0, 1, 2, 4, 3, 5)
        out.append(both.reshape(2, N_KV_HEADS, N_PAIRS * TILE_ROWS, 4 * TILE_ROWS))
    return jnp.asarray(np.stack(out), F32)


def _attn_kernel(q_ref, kvc_ref, kvp_ref, bias_ref, ag_ref, y_ref, kvr, n_scr, d_scr, m_scr):
    first_tile = (pl.program_id(2) == 0).astype(jnp.int32)
    lo = lax.broadcasted_iota(jnp.int32, (1, PAIR), 1) < HEAD_DIM
    ones_top = jnp.broadcast_to(jnp.where(lo, 1.0, 0.0), (2 * TILE_ROWS, PAIR))
    ones_rhs = jnp.concatenate([ones_top, 1.0 - ones_top], axis=0).astype(BF16)

    for s in range(N_SLABS):
        kvr[s, :TILE_ROWS] = pltpu.roll(kvp_ref[s], HEAD_DIM, 1)
        kvr[s, TILE_ROWS:] = pltpu.roll(kvc_ref[s], HEAD_DIM, 1)

    def block(p, rd, u):
        d, nm, c = PATTERNS[p]
        slabs = [rd + d * m for m in range(nm)]
        r0 = u * c
        qs = jnp.concatenate([q_ref[sl, r0:r0 + c, pr * PAIR:(pr + 1) * PAIR]
                              for pr in range(N_PAIRS) for sl in slabs], axis=0).astype(BF16)
        if u == 0:
            prev_a = [kvp_ref[sl, TILE_ROWS - c:TILE_ROWS, :] for sl in slabs]
        else:
            prev_a = [kvc_ref[sl, r0 - c:r0, :] for sl in slabs]
        kv_a = jnp.concatenate(prev_a + [kvc_ref[sl, r0:r0 + c, :] for sl in slabs], axis=0)
        kv_b = jnp.concatenate([kvr[sl, TILE_ROWS + r0 - c:TILE_ROWS + r0, :] for sl in slabs]
                               + [kvr[sl, TILE_ROWS + r0:TILE_ROWS + r0 + c, :] for sl in slabs],
                               axis=0)
        k_nt = jnp.concatenate([jnp.where(lo, kv_a, 0.0), jnp.where(lo, 0.0, kv_b)],
                               axis=0).astype(BF16)
        v_bd = jnp.concatenate([jnp.where(lo, kv_b, 0.0), jnp.where(lo, 0.0, kv_a)],
                               axis=0).astype(BF16)
        s = lax.dot_general(qs, k_nt, (((1,), (1,)), ((), ())), preferred_element_type=F32)
        s = s + (bias_ref[p, first_tile] if u == 0 else bias_ref[p, 0])
        sa, sb = s[:, :2 * TILE_ROWS], s[:, 2 * TILE_ROWS:]
        ma = jnp.max(sa, axis=-1, keepdims=True)
        mb = jnp.max(sb, axis=-1, keepdims=True)
        pe = jnp.concatenate([jnp.exp2(sa - ma), jnp.exp2(sb - mb)], axis=1).astype(BF16)
        rhs = jnp.concatenate([v_bd, ones_rhs], axis=1)
        r = jnp.dot(pe, rhs, preferred_element_type=F32)
        return r[:, :PAIR], r[:, PAIR:], jnp.where(lo, ma, mb)

    for p in range(len(PATTERNS) - 1):
        d, nm, c = PATTERNS[p]
        for blk in range(BLOCKS_PER_PATTERN):
            rd, u = blk // nm, blk % nm
            parts = block(p, rd, u)
            for pr in range(N_PAIRS):
                for m in range(nm):
                    rows = slice(pr * TILE_ROWS + m * c, pr * TILE_ROWS + (m + 1) * c)
                    lanes = slice(pr * PAIR, (pr + 1) * PAIR)
                    for scr, val in zip((n_scr, d_scr, m_scr), parts):
                        scr[p, rd + d * m, u * c:(u + 1) * c, lanes] = val[rows]

    p_last = len(PATTERNS) - 1
    assert PATTERNS[p_last][1:] == (1, TILE_ROWS)
    for slab in range(N_SLABS):
        num2, den2, m2 = [jnp.concatenate([v[:TILE_ROWS], v[TILE_ROWS:]], axis=1)
                          for v in block(p_last, slab, 0)]
        m0, m1 = m_scr[0, slab], m_scr[1, slab]
        mx = jnp.maximum(jnp.maximum(m0, m1), m2)
        e0, e1, e2 = jnp.exp2(m0 - mx), jnp.exp2(m1 - mx), jnp.exp2(m2 - mx)
        num = e0 * n_scr[0, slab] + e1 * n_scr[1, slab] + e2 * num2
        den = e0 * d_scr[0, slab] + e1 * d_scr[1, slab] + e2 * den2
        gate = ag_ref[slab].astype(F32)
        y_ref[slab] = (num / den * _silu(gate)).astype(y_ref.dtype)


def _attention(q, kv, ag, bias):
    b, _, rows, _ = q.shape
    n_tiles = rows // TILE_ROWS
    grid = (N_KV_HEADS, b, n_tiles)
    scratch = [pltpu.VMEM((N_SLABS, 2 * TILE_ROWS, GROUP_KV), F32)]
    scratch += [pltpu.VMEM((len(PATTERNS) - 1, N_SLABS, TILE_ROWS, GROUP_Q), F32)] * 3
    return pl.pallas_call(
        _attn_kernel,
        grid=grid,
        in_specs=[
            pl.BlockSpec((None, N_SLABS, TILE_ROWS, GROUP_Q), lambda g, bi, n: (bi, 0, n, g)),
            pl.BlockSpec((None, N_SLABS, TILE_ROWS, GROUP_KV), lambda g, bi, n: (bi, 0, n, g)),
            pl.BlockSpec((None, N_SLABS, TILE_ROWS, GROUP_KV),
                         lambda g, bi, n: (bi, 0, jnp.maximum(n - 1, 0), g)),
            pl.BlockSpec((3, 2, None, N_PAIRS * TILE_ROWS, 4 * TILE_ROWS),
                         lambda g, bi, n: (0, 0, g, 0, 0)),
            pl.BlockSpec((None, N_SLABS, TILE_ROWS, GROUP_Q), lambda g, bi, n: (bi, 0, n, g)),
        ],
        out_specs=pl.BlockSpec((None, N_SLABS, TILE_ROWS, GROUP_Q), lambda g, bi, n: (bi, 0, n, g)),
        out_shape=jax.ShapeDtypeStruct((b, N_SLABS, rows, D_MODEL), BF16),
        scratch_shapes=scratch,
        compiler_params=pltpu.CompilerParams(
            dimension_semantics=("arbitrary", "arbitrary", "arbitrary"),
            vmem_limit_bytes=VMEM_LIMIT),
        name="attention",
    )(q, kv, kv, bias, ag)


def _conv_kernel(cv_ref, cu_ref, cvp_ref, cup_ref, cg_ref, w_ref, b_ref, lg_ref, lb_ref, y_ref,
                 hs_ref, cs_ref):
    t = pl.program_id(1)
    tiles = CONV_ROWS // CONV_TILE
    tile_shift = tiles.bit_length() - 1

    def prep(rr, carry):
        hc = cv_ref[rr].astype(F32) * _sigmoid(cu_ref[rr].astype(F32))
        hp = cvp_ref[rr].astype(F32) * _sigmoid(cup_ref[rr].astype(F32))
        hp = jnp.where(t == 0, 0.0, hp)
        full = jnp.concatenate([hp, hc], axis=0)
        hs_ref[0, rr] = hc
        hs_ref[1, rr] = pltpu.roll(full, 1, 0)[PREV_ROWS:]
        hs_ref[2, rr] = pltpu.roll(full, 2, 0)[PREV_ROWS:]
        return carry

    lax.fori_loop(0, N_SLABS, prep, 0, unroll=2)

    def tap_body(rg, carry, lanes, r_base):
        row0 = pl.multiple_of(rg * SUBLANES, SUBLANES)
        accs = [jnp.broadcast_to(b_ref[:, lanes], (SUBLANES, LANES))] * TAP_SLABS
        for s in range(CONV_K):
            w8 = w_ref[CONV_K - 1 - s, :, lanes]
            for k in range(TAP_SLABS):
                r = r_base + k
                rr = (r - s) % N_SLABS
                down = (s - r + N_SLABS - 1) // N_SLABS
                accs[k] = accs[k] + w8 * hs_ref[down, rr, pl.ds(row0, SUBLANES), lanes]
        for k in range(TAP_SLABS):
            cs_ref[r_base + k, pl.ds(row0, SUBLANES), lanes] = accs[k]
        return carry

    for lt in range(COL_BLOCKS):
        lanes = slice(lt * LANES, (lt + 1) * LANES)
        for r_base in range(0, N_SLABS, TAP_SLABS):
            lax.fori_loop(0, CONV_ROWS // SUBLANES,
                          functools.partial(tap_body, lanes=lanes, r_base=r_base), 0)

    def ln_body(it, carry):
        r = jnp.right_shift(it, tile_shift)
        row0 = pl.multiple_of(jnp.bitwise_and(it, tiles - 1) * CONV_TILE, CONV_TILE)
        acc = cs_ref[r, pl.ds(row0, CONV_TILE), :]
        mu = jnp.mean(acc, axis=-1, keepdims=True)
        cen = acc - mu
        var = jnp.mean(cen * cen, axis=-1, keepdims=True)
        z = cen * lax.rsqrt(var + LN_EPS) * lg_ref[...] + lb_ref[...]
        gate = cg_ref[r, pl.ds(row0, CONV_TILE), :].astype(F32)
        y_ref[r, pl.ds(row0, CONV_TILE), :] = (_silu(z) * _silu(gate)).astype(y_ref.dtype)
        return carry

    lax.fori_loop(0, N_SLABS * tiles, ln_body, 0, unroll=4)


def _conv(cv, cu, cg, conv_w, conv_b, ln_g, ln_b):
    b, _, rows, _ = cv.shape
    grid = (b, rows // CONV_ROWS)
    cur = pl.BlockSpec((None, N_SLABS, CONV_ROWS, D_MODEL), lambda bi, t: (bi, 0, t, 0))
    prev = pl.BlockSpec((None, N_SLABS, PREV_ROWS, D_MODEL),
                        lambda bi, t: (bi, 0, jnp.maximum(t * (CONV_ROWS // PREV_ROWS) - 1, 0), 0))
    vec = pl.BlockSpec((1, D_MODEL), lambda bi, t: (0, 0))
    w_bcast = jnp.broadcast_to(conv_w[:, None, :], (CONV_K, SUBLANES, D_MODEL))
    return pl.pallas_call(
        _conv_kernel,
        grid=grid,
        in_specs=[cur, cur, prev, prev, cur,
                  pl.BlockSpec((CONV_K, SUBLANES, D_MODEL), lambda bi, t: (0, 0, 0)), vec, vec, vec],
        out_specs=cur,
        out_shape=jax.ShapeDtypeStruct(cv.shape, BF16),
        scratch_shapes=[pltpu.VMEM((3, N_SLABS, CONV_ROWS, D_MODEL), F32),
                        pltpu.VMEM((N_SLABS, CONV_ROWS, D_MODEL), F32)],
        compiler_params=pltpu.CompilerParams(
            dimension_semantics=("arbitrary", "arbitrary"),
            vmem_limit_bytes=VMEM_LIMIT),
        name="conv",
    )(cv, cu, cv, cu, cg, w_bcast, conv_b, ln_g, ln_b)


def _outproj_kernel(*refs):
    x_refs = refs[:COL_BLOCKS]
    ya_ref, yc_ref, wa_ref, wc_ref, g_ref, o_ref, stage = refs[COL_BLOCKS:]
    ya = ya_ref[...].reshape(N_SLABS * OUT_ROWS, D_MODEL)
    yc = yc_ref[...].reshape(N_SLABS * OUT_ROWS, D_MODEL)
    half = D_MODEL // 2
    deltas = [jnp.dot(ya, wa_ref[:, c:c + half], preferred_element_type=F32)
              + jnp.dot(yc, wc_ref[:, c:c + half], preferred_element_type=F32)
              for c in (0, half)]
    for r in range(N_SLABS):
        rows = slice(r * OUT_ROWS, (r + 1) * OUT_ROWS)
        xs = _slab_rows(x_refs, r, OUT_ROWS) + jnp.concatenate([dl[rows] for dl in deltas], axis=1)
        ms = jnp.mean(xs * xs, axis=-1, keepdims=True)
        res = xs * lax.rsqrt(ms + NORM_EPS) * g_ref[...]
        for cb in range(COL_BLOCKS):
            stage[cb, pl.ds(r, OUT_ROWS, stride=N_SLABS), :] = res[:, cb * LANES:(cb + 1) * LANES]
    for cb in range(COL_BLOCKS):
        o_ref[:, cb * LANES:(cb + 1) * LANES] = stage[cb]


def _outproj(ya, yc, x, w_att, w_conv, final_g):
    b, seq, _ = x.shape
    rows = seq // N_SLABS
    tok = OUT_ROWS * N_SLABS
    grid = (b, rows // OUT_ROWS)
    slab = pl.BlockSpec((None, N_SLABS, OUT_ROWS, D_MODEL), lambda bi, t: (bi, 0, t, 0))
    wspec = pl.BlockSpec((D_MODEL, D_MODEL), lambda bi, t: (0, 0))

    def x_spec(cb):
        return pl.BlockSpec((None, tok, LANES), lambda bi, t: (bi, t, cb))

    return pl.pallas_call(
        _outproj_kernel,
        grid=grid,
        in_specs=[x_spec(cb) for cb in range(COL_BLOCKS)] + [
            slab, slab, wspec, wspec, pl.BlockSpec((1, D_MODEL), lambda bi, t: (0, 0))],
        out_specs=pl.BlockSpec((None, tok, D_MODEL), lambda bi, t: (bi, t, 0)),
        out_shape=jax.ShapeDtypeStruct(x.shape, F32),
        scratch_shapes=[pltpu.VMEM((COL_BLOCKS, tok, LANES), F32)],
        compiler_params=pltpu.CompilerParams(
            dimension_semantics=("arbitrary", "arbitrary"),
            vmem_limit_bytes=VMEM_LIMIT),
        name="outproj",
    )(*([x] * COL_BLOCKS), ya, yc, w_att, w_conv, final_g)


def kernel(x, norm_g, w_in, conv_w, conv_b, conv_ln_g, conv_ln_b, w_out, final_norm_g):
    b, seq, dm = x.shape
    assert dm == D_MODEL and seq % (N_SLABS * TILE_ROWS) == 0
    assert norm_g.shape[0] == 1, "single layer"
    q, kv, ag, cv, cu, cg = _inproj(x, norm_g[0][None], w_in[0].astype(BF16))
    y_att = _attention(q, kv, ag, _attn_bias())
    y_conv = _conv(cv, cu, cg, conv_w[0], conv_b[0][None], conv_ln_g[0][None], conv_ln_b[0][None])
    w_o = w_out[0].astype(BF16)
    return _outproj(y_att, y_conv, x, w_o[:D_MODEL], w_o[D_MODEL:], final_norm_g[None])
```

```python
import functools
import math

import jax
import jax.numpy as jnp
import numpy as np
from jax import lax
from jax.experimental import pallas as pl
from jax.experimental.pallas import tpu as pltpu

F32 = jnp.float32
BF16 = jnp.bfloat16

D_MODEL = 1024
HEAD_DIM = 64
N_Q_HEADS = 16
N_KV_HEADS = 4
Q_PER_KV = N_Q_HEADS // N_KV_HEADS
KV_WIDTH = N_KV_HEADS * HEAD_DIM
CONV_K = 31
WINDOW_STEPS = 128
NORM_EPS = 1e-6
LN_EPS = 1e-5
LOG2E = math.log2(math.e)

LANES = 128
SUBLANES = 8
N_SLABS = 16
TILE_ROWS = 128
PATTERNS = ((1, 16, 8), (4, 4, 32), (16, 1, 128))
BLOCKS_PER_PATTERN = 16
GROUP_Q = Q_PER_KV * HEAD_DIM
GROUP_KV = 2 * HEAD_DIM
PAIR = 2 * HEAD_DIM
N_PAIRS = Q_PER_KV // 2
COL_BLOCKS = D_MODEL // LANES

IN_ROWS = 32
CONV_ROWS = 64
CONV_TILE = 16
TAP_SLABS = 16
PREV_ROWS = 16
OUT_ROWS = 64
VMEM_LIMIT = 56 * 1024 * 1024


def _sigmoid(x):
    return 1.0 / (1.0 + jnp.exp(-x))


def _silu(x):
    return x * _sigmoid(x)


def _slab_rows(x_refs, r, rows):
    return jnp.concatenate([xr[pl.ds(r, rows, stride=N_SLABS), :] for xr in x_refs], axis=-1)


def _inproj_kernel(*refs):
    x_refs = refs[:COL_BLOCKS]
    ng_ref, w_ref = refs[COL_BLOCKS:COL_BLOCKS + 2]
    q_ref, kv_ref, ag_ref, cv_ref, cu_ref, cg_ref = refs[COL_BLOCKS + 2:]
    hs = []
    for r in range(N_SLABS):
        xs = _slab_rows(x_refs, r, IN_ROWS)
        ms = jnp.mean(xs * xs, axis=-1, keepdims=True)
        hs.append((xs * lax.rsqrt(ms + NORM_EPS) * ng_ref[...]).astype(BF16))
    h = jnp.concatenate(hs, axis=0)

    def emit(ref, c0, width, scale):
        for cc in range(0, width, 512):
            wc = min(512, width - cc)
            res = jnp.dot(h, w_ref[:, c0 + cc:c0 + cc + wc], preferred_element_type=F32)
            if scale is not None:
                res = res * scale
            for r in range(N_SLABS):
                ref[r, :, cc:cc + wc] = res[r * IN_ROWS:(r + 1) * IN_ROWS].astype(ref.dtype)

    c = 0
    emit(q_ref, c, D_MODEL, HEAD_DIM ** -0.5 * LOG2E)
    c += D_MODEL
    res_k = jnp.dot(h, w_ref[:, c:c + KV_WIDTH], preferred_element_type=F32)
    res_v = jnp.dot(h, w_ref[:, c + KV_WIDTH:c + 2 * KV_WIDTH], preferred_element_type=F32)
    heads = [res[:, g * HEAD_DIM:(g + 1) * HEAD_DIM] for g in range(N_KV_HEADS) for res in (res_k, res_v)]
    res_kv = jnp.concatenate(heads, axis=1)
    for r in range(N_SLABS):
        kv_ref[r] = res_kv[r * IN_ROWS:(r + 1) * IN_ROWS]
    c += 2 * KV_WIDTH
    for ref in (ag_ref, cv_ref, cu_ref, cg_ref):
        emit(ref, c, D_MODEL, None)
        c += D_MODEL


def _inproj(x, norm_g, w_perm):
    b, seq, _ = x.shape
    rows = seq // N_SLABS
    grid = (b, rows // IN_ROWS)

    def slab_out(width, dtype):
        return (jax.ShapeDtypeStruct((b, N_SLABS, rows, width), dtype),
                pl.BlockSpec((None, N_SLABS, IN_ROWS, width), lambda bi, j: (bi, 0, j, 0)))

    def x_spec(cb):
        return pl.BlockSpec((None, IN_ROWS * N_SLABS, LANES), lambda bi, j: (bi, j, cb))

    outs = [slab_out(D_MODEL, F32), slab_out(2 * KV_WIDTH, F32)] + [slab_out(D_MODEL, BF16)] * 4
    return pl.pallas_call(
        _inproj_kernel,
        grid=grid,
        in_specs=[x_spec(cb) for cb in range(COL_BLOCKS)] + [
            pl.BlockSpec((1, D_MODEL), lambda bi, j: (0, 0)),
            pl.BlockSpec(w_perm.shape, lambda bi, j: (0, 0))],
        out_specs=[o[1] for o in outs],
        out_shape=[o[0] for o in outs],
        compiler_params=pltpu.CompilerParams(
            dimension_semantics=("arbitrary", "arbitrary"),
            vmem_limit_bytes=VMEM_LIMIT),
        name="inproj",
    )(*([x] * COL_BLOCKS), norm_g, w_perm)


def _attn_bias():
    slopes = np.exp2(-8.0 * (np.arange(N_Q_HEADS, dtype=np.float32) + 1.0) / N_Q_HEADS)
    out = []
    for d, nm, c in PATTERNS:
        a = np.arange(TILE_ROWS)
        jq = (a % c) * nm + a // c
        col = np.arange(2 * TILE_ROWS)
        part, rem = col // TILE_ROWS, col % TILE_ROWS
        jk = (rem % c) * nm + rem // c - (1 - part) * TILE_ROWS
        dist = jq[:, None] - jk[None, :]
        valid = (dist >= 0) & (dist <= WINDOW_STEPS)
        bias = -slopes[:, None, None] * (dist * d).astype(np.float32)[None] * np.float32(LOG2E)
        v0 = np.where(valid[None], bias, -np.inf)
        v1 = np.where((valid & (part[None, :] == 1))[None], bias, -np.inf)
        both = np.stack([v0, v1])
        both = both.reshape(2, N_KV_HEADS, N_PAIRS, 2, TILE_ROWS, 2 * TILE_ROWS)
        both = both.transpose(0, 1, 2, 4, 3, 5)
        out.append(both.reshape(2, N_KV_HEADS, N_PAIRS * TILE_ROWS, 4 * TILE_ROWS))
    return jnp.asarray(np.stack(out), F32)


def _attn_kernel(q_ref, kvc_ref, kvp_ref, bias_ref, ag_ref, y_ref, kvr, n_scr, d_scr, m_scr):
    first_tile = (pl.program_id(2) == 0).astype(jnp.int32)
    lo = lax.broadcasted_iota(jnp.int32, (1, PAIR), 1) < HEAD_DIM
    ones_top = jnp.broadcast_to(jnp.where(lo, 1.0, 0.0), (2 * TILE_ROWS, PAIR))
    ones_rhs = jnp.concatenate([ones_top, 1.0 - ones_top], axis=0).astype(BF16)

    for s in range(N_SLABS):
        kvr[s, :TILE_ROWS] = pltpu.roll(kvp_ref[s], HEAD_DIM, 1)
        kvr[s, TILE_ROWS:] = pltpu.roll(kvc_ref[s], HEAD_DIM, 1)

    def block(p, rd, u):
        d, nm, c = PATTERNS[p]
        slabs = [rd + d * m for m in range(nm)]
        r0 = u * c
        qs = jnp.concatenate([q_ref[sl, r0:r0 + c, pr * PAIR:(pr + 1) * PAIR]
                              for pr in range(N_PAIRS) for sl in slabs], axis=0).astype(BF16)
        if u == 0:
            prev_a = [kvp_ref[sl, TILE_ROWS - c:TILE_ROWS, :] for sl in slabs]
        else:
            prev_a = [kvc_ref[sl, r0 - c:r0, :] for sl in slabs]
        kv_a = jnp.concatenate(prev_a + [kvc_ref[sl, r0:r0 + c, :] for sl in slabs], axis=0)
        kv_b = jnp.concatenate([kvr[sl, TILE_ROWS + r0 - c:TILE_ROWS + r0, :] for sl in slabs]
                               + [kvr[sl, TILE_ROWS + r0:TILE_ROWS + r0 + c, :] for sl in slabs],
                               axis=0)
        k_nt = jnp.concatenate([jnp.where(lo, kv_a, 0.0), jnp.where(lo, 0.0, kv_b)],
                               axis=0).astype(BF16)
        v_bd = jnp.concatenate([jnp.where(lo, kv_b, 0.0), jnp.where(lo, 0.0, kv_a)],
                               axis=0).astype(BF16)
        s = lax.dot_general(qs, k_nt, (((1,), (1,)), ((), ())), preferred_element_type=F32)
        s = s + (bias_ref[p, first_tile] if u == 0 else bias_ref[p, 0])
        sa, sb = s[:, :2 * TILE_ROWS], s[:, 2 * TILE_ROWS:]
        ma = jnp.max(sa, axis=-1, keepdims=True)
        mb = jnp.max(sb, axis=-1, keepdims=True)
        pe = jnp.concatenate([jnp.exp2(sa - ma), jnp.exp2(sb - mb)], axis=1).astype(BF16)
        rhs = jnp.concatenate([v_bd, ones_rhs], axis=1)
        r = jnp.dot(pe, rhs, preferred_element_type=F32)
        return r[:, :PAIR], r[:, PAIR:], jnp.where(lo, ma, mb)

    def merge(old, new):
        (n0, d0, m0), (n1, d1, m1) = old, new
        mx = jnp.maximum(m0, m1)
        e0, e1 = jnp.exp2(m0 - mx), jnp.exp2(m1 - mx)
        return e0 * n0 + e1 * n1, e0 * d0 + e1 * d1, mx

    scrs = (n_scr, d_scr, m_scr)
    for p in range(len(PATTERNS) - 1):
        d, nm, c = PATTERNS[p]
        for blk in range(BLOCKS_PER_PATTERN):
            rd, u = blk // nm, blk % nm
            parts = block(p, rd, u)
            where = [(rd + d * m, slice(u * c, (u + 1) * c), slice(pr * PAIR, (pr + 1) * PAIR))
                     for pr in range(N_PAIRS) for m in range(nm)]
            if p > 0:
                old = [jnp.concatenate([scr[w] for w in where], axis=0) for scr in scrs]
                parts = merge(old, parts)
            for k, w in enumerate(where):
                for scr, val in zip(scrs, parts):
                    scr[w] = val[k * c:(k + 1) * c]

    p_last = len(PATTERNS) - 1
    assert PATTERNS[p_last][1:] == (1, TILE_ROWS)
    for slab in range(N_SLABS):
        new = [jnp.concatenate([v[:TILE_ROWS], v[TILE_ROWS:]], axis=1)
               for v in block(p_last, slab, 0)]
        num, den, _ = merge([scr[slab] for scr in scrs], new)
        gate = ag_ref[slab].astype(F32)
        y_ref[slab] = (num / den * _silu(gate)).astype(y_ref.dtype)


def _attention(q, kv, ag, bias):
    b, _, rows, _ = q.shape
    n_tiles = rows // TILE_ROWS
    grid = (N_KV_HEADS, b, n_tiles)
    scratch = [pltpu.VMEM((N_SLABS, 2 * TILE_ROWS, GROUP_KV), F32)]
    scratch += [pltpu.VMEM((N_SLABS, TILE_ROWS, GROUP_Q), F32)] * 3
    return pl.pallas_call(
        _attn_kernel,
        grid=grid,
        in_specs=[
            pl.BlockSpec((None, N_SLABS, TILE_ROWS, GROUP_Q), lambda g, bi, n: (bi, 0, n, g)),
            pl.BlockSpec((None, N_SLABS, TILE_ROWS, GROUP_KV), lambda g, bi, n: (bi, 0, n, g)),
            pl.BlockSpec((None, N_SLABS, TILE_ROWS, GROUP_KV),
                         lambda g, bi, n: (bi, 0, jnp.maximum(n - 1, 0), g)),
            pl.BlockSpec((3, 2, None, N_PAIRS * TILE_ROWS, 4 * TILE_ROWS),
                         lambda g, bi, n: (0, 0, g, 0, 0)),
            pl.BlockSpec((None, N_SLABS, TILE_ROWS, GROUP_Q), lambda g, bi, n: (bi, 0, n, g)),
        ],
        out_specs=pl.BlockSpec((None, N_SLABS, TILE_ROWS, GROUP_Q), lambda g, bi, n: (bi, 0, n, g)),
        out_shape=jax.ShapeDtypeStruct((b, N_SLABS, rows, D_MODEL), BF16),
        scratch_shapes=scratch,
        compiler_params=pltpu.CompilerParams(
            dimension_semantics=("arbitrary", "arbitrary", "arbitrary"),
            vmem_limit_bytes=VMEM_LIMIT),
        name="attention",
    )(q, kv, kv, bias, ag)


def _conv_kernel(cv_ref, cu_ref, cvp_ref, cup_ref, cg_ref, w_ref, b_ref, lg_ref, lb_ref, y_ref,
                 hs_ref, cs_ref):
    t = pl.program_id(1)
    tiles = CONV_ROWS // CONV_TILE
    tile_shift = tiles.bit_length() - 1

    def prep(rr, carry):
        hc = cv_ref[rr].astype(F32) * _sigmoid(cu_ref[rr].astype(F32))
        hp = cvp_ref[rr].astype(F32) * _sigmoid(cup_ref[rr].astype(F32))
        hp = jnp.where(t == 0, 0.0, hp)
        full = jnp.concatenate([hp, hc], axis=0)
        hs_ref[0, rr] = hc
        hs_ref[1, rr] = pltpu.roll(full, 1, 0)[PREV_ROWS:]
        hs_ref[2, rr] = pltpu.roll(full, 2, 0)[PREV_ROWS:]
        return carry

    lax.fori_loop(0, N_SLABS, prep, 0, unroll=2)

    def tap_body(rg, carry, lanes, r_base):
        row0 = pl.multiple_of(rg * SUBLANES, SUBLANES)
        accs = [jnp.broadcast_to(b_ref[:, lanes], (SUBLANES, LANES))] * TAP_SLABS
        for s in range(CONV_K):
            w8 = w_ref[CONV_K - 1 - s, :, lanes]
            for k in range(TAP_SLABS):
                r = r_base + k
                rr = (r - s) % N_SLABS
                down = (s - r + N_SLABS - 1) // N_SLABS
                accs[k] = accs[k] + w8 * hs_ref[down, rr, pl.ds(row0, SUBLANES), lanes]
        for k in range(TAP_SLABS):
            cs_ref[r_base + k, pl.ds(row0, SUBLANES), lanes] = accs[k]
        return carry

    for lt in range(COL_BLOCKS):
        lanes = slice(lt * LANES, (lt + 1) * LANES)
        for r_base in range(0, N_SLABS, TAP_SLABS):
            lax.fori_loop(0, CONV_ROWS // SUBLANES,
                          functools.partial(tap_body, lanes=lanes, r_base=r_base), 0)

    def ln_body(it, carry):
        r = jnp.right_shift(it, tile_shift)
        row0 = pl.multiple_of(jnp.bitwise_and(it, tiles - 1) * CONV_TILE, CONV_TILE)
        acc = cs_ref[r, pl.ds(row0, CONV_TILE), :]
        mu = jnp.mean(acc, axis=-1, keepdims=True)
        cen = acc - mu
        var = jnp.mean(cen * cen, axis=-1, keepdims=True)
        z = cen * lax.rsqrt(var + LN_EPS) * lg_ref[...] + lb_ref[...]
        gate = cg_ref[r, pl.ds(row0, CONV_TILE), :].astype(F32)
        y_ref[r, pl.ds(row0, CONV_TILE), :] = (_silu(z) * _silu(gate)).astype(y_ref.dtype)
        return carry

    lax.fori_loop(0, N_SLABS * tiles, ln_body, 0, unroll=4)


def _conv(cv, cu, cg, conv_w, conv_b, ln_g, ln_b):
    b, _, rows, _ = cv.shape
    grid = (b, rows // CONV_ROWS)
    cur = pl.BlockSpec((None, N_SLABS, CONV_ROWS, D_MODEL), lambda bi, t: (bi, 0, t, 0))
    prev = pl.BlockSpec((None, N_SLABS, PREV_ROWS, D_MODEL),
                        lambda bi, t: (bi, 0, jnp.maximum(t * (CONV_ROWS // PREV_ROWS) - 1, 0), 0))
    vec = pl.BlockSpec((1, D_MODEL), lambda bi, t: (0, 0))
    w_bcast = jnp.broadcast_to(conv_w[:, None, :], (CONV_K, SUBLANES, D_MODEL))
    return pl.pallas_call(
        _conv_kernel,
        grid=grid,
        in_specs=[cur, cur, prev, prev, cur,
                  pl.BlockSpec((CONV_K, SUBLANES, D_MODEL), lambda bi, t: (0, 0, 0)), vec, vec, vec],
        out_specs=cur,
        out_shape=jax.ShapeDtypeStruct(cv.shape, BF16),
        scratch_shapes=[pltpu.VMEM((3, N_SLABS, CONV_ROWS, D_MODEL), F32),
                        pltpu.VMEM((N_SLABS, CONV_ROWS, D_MODEL), F32)],
        compiler_params=pltpu.CompilerParams(
            dimension_semantics=("arbitrary", "arbitrary"),
            vmem_limit_bytes=VMEM_LIMIT),
        name="conv",
    )(cv, cu, cv, cu, cg, w_bcast, conv_b, ln_g, ln_b)


def _outproj_kernel(*refs):
    x_refs = refs[:COL_BLOCKS]
    ya_ref, yc_ref, wa_ref, wc_ref, g_ref, o_ref, stage = refs[COL_BLOCKS:]
    ya = ya_ref[...].reshape(N_SLABS * OUT_ROWS, D_MODEL)
    yc = yc_ref[...].reshape(N_SLABS * OUT_ROWS, D_MODEL)
    half = D_MODEL // 2
    deltas = [jnp.dot(ya, wa_ref[:, c:c + half], preferred_element_type=F32)
              + jnp.dot(yc, wc_ref[:, c:c + half], preferred_element_type=F32)
              for c in (0, half)]
    for r in range(N_SLABS):
        rows = slice(r * OUT_ROWS, (r + 1) * OUT_ROWS)
        xs = _slab_rows(x_refs, r, OUT_ROWS) + jnp.concatenate([dl[rows] for dl in deltas], axis=1)
        ms = jnp.mean(xs * xs, axis=-1, keepdims=True)
        res = xs * lax.rsqrt(ms + NORM_EPS) * g_ref[...]
        for cb in range(COL_BLOCKS):
            stage[cb, pl.ds(r, OUT_ROWS, stride=N_SLABS), :] = res[:, cb * LANES:(cb + 1) * LANES]
    for cb in range(COL_BLOCKS):
        o_ref[:, cb * LANES:(cb + 1) * LANES] = stage[cb]


def _outproj(ya, yc, x, w_att, w_conv, final_g):
    b, seq, _ = x.shape
    rows = seq // N_SLABS
    tok = OUT_ROWS * N_SLABS
    grid = (b, rows // OUT_ROWS)
    slab = pl.BlockSpec((None, N_SLABS, OUT_ROWS, D_MODEL), lambda bi, t: (bi, 0, t, 0))
    wspec = pl.BlockSpec((D_MODEL, D_MODEL), lambda bi, t: (0, 0))

    def x_spec(cb):
        return pl.BlockSpec((None, tok, LANES), lambda bi, t: (bi, t, cb))

    return pl.pallas_call(
        _outproj_kernel,
        grid=grid,
        in_specs=[x_spec(cb) for cb in range(COL_BLOCKS)] + [
            slab, slab, wspec, wspec, pl.BlockSpec((1, D_MODEL), lambda bi, t: (0, 0))],
        out_specs=pl.BlockSpec((None, tok, D_MODEL), lambda bi, t: (bi, t, 0)),
        out_shape=jax.ShapeDtypeStruct(x.shape, F32),
        scratch_shapes=[pltpu.VMEM((COL_BLOCKS, tok, LANES), F32)],
        compiler_params=pltpu.CompilerParams(
            dimension_semantics=("arbitrary", "arbitrary"),
            vmem_limit_bytes=VMEM_LIMIT),
        name="outproj",
    )(*([x] * COL_BLOCKS), ya, yc, w_att, w_conv, final_g)


def kernel(x, norm_g, w_in, conv_w, conv_b, conv_ln_g, conv_ln_b, w_out, final_norm_g):
    b, seq, dm = x.shape
    assert dm == D_MODEL and seq % (N_SLABS * TILE_ROWS) == 0
    assert norm_g.shape[0] == 1, "single layer"
    q, kv, ag, cv, cu, cg = _inproj(x, norm_g[0][None], w_in[0].astype(BF16))
    y_att = _attention(q, kv, ag, _attn_bias())
    y_conv = _conv(cv, cu, cg, conv_w[0], conv_b[0][None], conv_ln_g[0][None], conv_ln_b[0][None])
    w_o = w_out[0].astype(BF16)
    return _outproj(y_att, y_conv, x, w_o[:D_MODEL], w_o[D_MODEL:], final_norm_g[None])
```

```python
import functools
import math

import jax
import jax.numpy as jnp
import numpy as np
from jax import lax
from jax.experimental import pallas as pl
from jax.experimental.pallas import tpu as pltpu

F32 = jnp.float32
BF16 = jnp.bfloat16

D_MODEL = 1024
HEAD_DIM = 64
N_Q_HEADS = 16
N_KV_HEADS = 4
Q_PER_KV = N_Q_HEADS // N_KV_HEADS
KV_WIDTH = N_KV_HEADS * HEAD_DIM
CONV_K = 31
WINDOW_STEPS = 128
NORM_EPS = 1e-6
LN_EPS = 1e-5
LOG2E = math.log2(math.e)

LANES = 128
SUBLANES = 8
N_SLABS = 16
TILE_ROWS = 128
PATTERNS = ((1, 16, 8), (4, 4, 32), (16, 1, 128))
BLOCKS_PER_PATTERN = 16
GROUP_Q = Q_PER_KV * HEAD_DIM
GROUP_KV = 2 * HEAD_DIM
PAIR = 2 * HEAD_DIM
N_PAIRS = Q_PER_KV // 2
COL_BLOCKS = D_MODEL // LANES

IN_ROWS = 32
CONV_ROWS = 64
CONV_TILE = 16
TAP_SLABS = 16
PREV_ROWS = 16
OUT_ROWS = 64
VMEM_LIMIT = 56 * 1024 * 1024


def _sigmoid(x):
    return 1.0 / (1.0 + jnp.exp(-x))


def _silu(x):
    return x * _sigmoid(x)


def _slab_rows(x_refs, r, rows):
    return jnp.concatenate([xr[pl.ds(r, rows, stride=N_SLABS), :] for xr in x_refs], axis=-1)


def _inproj_kernel(*refs):
    x_refs = refs[:COL_BLOCKS]
    ng_ref, w_ref = refs[COL_BLOCKS:COL_BLOCKS + 2]
    q_ref, kv_ref, ag_ref, cv_ref, cu_ref, cg_ref = refs[COL_BLOCKS + 2:]
    hs = []
    for r in range(N_SLABS):
        xs = _slab_rows(x_refs, r, IN_ROWS)
        ms = jnp.mean(xs * xs, axis=-1, keepdims=True)
        hs.append((xs * lax.rsqrt(ms + NORM_EPS) * ng_ref[...]).astype(BF16))
    h = jnp.concatenate(hs, axis=0)

    def emit(ref, c0, width, scale):
        for cc in range(0, width, 512):
            wc = min(512, width - cc)
            res = jnp.dot(h, w_ref[:, c0 + cc:c0 + cc + wc], preferred_element_type=F32)
            if scale is not None:
                res = res * scale
            for r in range(N_SLABS):
                ref[r, :, cc:cc + wc] = res[r * IN_ROWS:(r + 1) * IN_ROWS].astype(ref.dtype)

    c = 0
    emit(q_ref, c, D_MODEL, HEAD_DIM ** -0.5 * LOG2E)
    c += D_MODEL
    res_k = jnp.dot(h, w_ref[:, c:c + KV_WIDTH], preferred_element_type=F32)
    res_v = jnp.dot(h, w_ref[:, c + KV_WIDTH:c + 2 * KV_WIDTH], preferred_element_type=F32)
    heads = [res[:, g * HEAD_DIM:(g + 1) * HEAD_DIM] for g in range(N_KV_HEADS) for res in (res_k, res_v)]
    res_kv = jnp.concatenate(heads, axis=1)
    for r in range(N_SLABS):
        kv_ref[r] = res_kv[r * IN_ROWS:(r + 1) * IN_ROWS]
    c += 2 * KV_WIDTH
    for ref in (ag_ref, cv_ref, cu_ref, cg_ref):
        emit(ref, c, D_MODEL, None)
        c += D_MODEL


def _inproj(x, norm_g, w_perm):
    b, seq, _ = x.shape
    rows = seq // N_SLABS
    grid = (b, rows // IN_ROWS)

    def slab_out(width, dtype):
        return (jax.ShapeDtypeStruct((b, N_SLABS, rows, width), dtype),
                pl.BlockSpec((None, N_SLABS, IN_ROWS, width), lambda bi, j: (bi, 0, j, 0)))

    def x_spec(cb):
        return pl.BlockSpec((None, IN_ROWS * N_SLABS, LANES), lambda bi, j: (bi, j, cb))

    outs = [slab_out(D_MODEL, F32), slab_out(2 * KV_WIDTH, F32)] + [slab_out(D_MODEL, BF16)] * 4
    return pl.pallas_call(
        _inproj_kernel,
        grid=grid,
        in_specs=[x_spec(cb) for cb in range(COL_BLOCKS)] + [
            pl.BlockSpec((1, D_MODEL), lambda bi, j: (0, 0)),
            pl.BlockSpec(w_perm.shape, lambda bi, j: (0, 0))],
        out_specs=[o[1] for o in outs],
        out_shape=[o[0] for o in outs],
        compiler_params=pltpu.CompilerParams(
            dimension_semantics=("arbitrary", "arbitrary"),
            vmem_limit_bytes=VMEM_LIMIT),
        name="inproj",
    )(*([x] * COL_BLOCKS), norm_g, w_perm)


def _attn_bias():
    slopes = np.exp2(-8.0 * (np.arange(N_Q_HEADS, dtype=np.float32) + 1.0) / N_Q_HEADS)
    out = []
    for d, nm, c in PATTERNS:
        a = np.arange(TILE_ROWS)
        jq = (a % c) * nm + a // c
        col = np.arange(2 * TILE_ROWS)
        part, rem = col // TILE_ROWS, col % TILE_ROWS
        jk = (rem % c) * nm + rem // c - (1 - part) * TILE_ROWS
        dist = jq[:, None] - jk[None, :]
        valid = (dist >= 0) & (dist <= WINDOW_STEPS)
        bias = -slopes[:, None, None] * (dist * d).astype(np.float32)[None] * np.float32(LOG2E)
        v0 = np.where(valid[None], bias, -np.inf)
        v1 = np.where((valid & (part[None, :] == 1))[None], bias, -np.inf)
        both = np.stack([v0, v1])
        both = both.reshape(2, N_KV_HEADS, N_PAIRS, 2, TILE_ROWS, 2 * TILE_ROWS)
        both = both.transpose(0, 1, 2, 4, 3, 5)
        out.append(both.reshape(2, N_KV_HEADS, N_PAIRS * TILE_ROWS, 4 * TILE_ROWS))
    return jnp.asarray(np.stack(out), F32)


def _attn_kernel(q_ref, kvc_ref, kvp_ref, bias_ref, ag_ref, y_ref, kvr, n_scr, d_scr, m_scr):
    first_tile = (pl.program_id(2) == 0).astype(jnp.int32)
    lo = lax.broadcasted_iota(jnp.int32, (1, PAIR), 1) < HEAD_DIM
    ones_top = jnp.broadcast_to(jnp.where(lo, 1.0, 0.0), (2 * TILE_ROWS, PAIR))
    ones_rhs = jnp.concatenate([ones_top, 1.0 - ones_top], axis=0).astype(BF16)

    for s in range(N_SLABS):
        kvr[s, :TILE_ROWS] = pltpu.roll(kvp_ref[s], HEAD_DIM, 1)
        kvr[s, TILE_ROWS:] = pltpu.roll(kvc_ref[s], HEAD_DIM, 1)

    def block(p, rd, u):
        d, nm, c = PATTERNS[p]
        slabs = [rd + d * m for m in range(nm)]
        r0 = u * c
        qs = jnp.concatenate([q_ref[sl, r0:r0 + c, pr * PAIR:(pr + 1) * PAIR]
                              for pr in range(N_PAIRS) for sl in slabs], axis=0).astype(BF16)
        if u == 0:
            prev_a = [kvp_ref[sl, TILE_ROWS - c:TILE_ROWS, :] for sl in slabs]
        else:
            prev_a = [kvc_ref[sl, r0 - c:r0, :] for sl in slabs]
        kv_a = jnp.concatenate(prev_a + [kvc_ref[sl, r0:r0 + c, :] for sl in slabs], axis=0)
        kv_b = jnp.concatenate([kvr[sl, TILE_ROWS + r0 - c:TILE_ROWS + r0, :] for sl in slabs]
                               + [kvr[sl, TILE_ROWS + r0:TILE_ROWS + r0 + c, :] for sl in slabs],
                               axis=0)
        k_nt = jnp.concatenate([jnp.where(lo, kv_a, 0.0), jnp.where(lo, 0.0, kv_b)],
                               axis=0).astype(BF16)
        v_bd = jnp.concatenate([jnp.where(lo, kv_b, 0.0), jnp.where(lo, 0.0, kv_a)],
                               axis=0).astype(BF16)
        s = lax.dot_general(qs, k_nt, (((1,), (1,)), ((), ())), preferred_element_type=F32)
        s = s + (bias_ref[p, first_tile] if u == 0 else bias_ref[p, 0])
        sa, sb = s[:, :2 * TILE_ROWS], s[:, 2 * TILE_ROWS:]
        ma = jnp.max(sa, axis=-1, keepdims=True)
        mb = jnp.max(sb, axis=-1, keepdims=True)
        pe = jnp.concatenate([jnp.exp2(sa - ma), jnp.exp2(sb - mb)], axis=1).astype(BF16)
        rhs = jnp.concatenate([v_bd, ones_rhs], axis=1)
        r = jnp.dot(pe, rhs, preferred_element_type=F32)
        return r[:, :PAIR], r[:, PAIR:], jnp.where(lo, ma, mb)

    def merge(old, new):
        (n0, d0, m0), (n1, d1, m1) = old, new
        mx = jnp.maximum(m0, m1)
        e0, e1 = jnp.exp2(m0 - mx), jnp.exp2(m1 - mx)
        return e0 * n0 + e1 * n1, e0 * d0 + e1 * d1, mx

    scrs = (n_scr, d_scr, m_scr)
    for p in range(len(PATTERNS) - 1):
        d, nm, c = PATTERNS[p]
        for blk in range(BLOCKS_PER_PATTERN):
            rd, u = blk // nm, blk % nm
            parts = block(p, rd, u)
            where = [(rd + d * m, slice(u * c, (u + 1) * c), slice(pr * PAIR, (pr + 1) * PAIR))
                     for pr in range(N_PAIRS) for m in range(nm)]
            if p > 0:
                old = [jnp.concatenate([scr[w] for w in where], axis=0) for scr in scrs]
                parts = merge(old, parts)
            for k, w in enumerate(where):
                for scr, val in zip(scrs, parts):
                    scr[w] = val[k * c:(k + 1) * c]

    p_last = len(PATTERNS) - 1
    assert PATTERNS[p_last][1:] == (1, TILE_ROWS)
    for slab in range(N_SLABS):
        new = [jnp.concatenate([v[:TILE_ROWS], v[TILE_ROWS:]], axis=1)
               for v in block(p_last, slab, 0)]
        num, den, _ = merge([scr[slab] for scr in scrs], new)
        gate = ag_ref[slab].astype(F32)
        y_ref[slab] = (num / den * _silu(gate)).astype(y_ref.dtype)


def _attention(q, kv, ag, bias):
    b, _, rows, _ = q.shape
    n_tiles = rows // TILE_ROWS
    grid = (N_KV_HEADS, b, n_tiles)
    scratch = [pltpu.VMEM((N_SLABS, 2 * TILE_ROWS, GROUP_KV), F32)]
    scratch += [pltpu.VMEM((N_SLABS, TILE_ROWS, GROUP_Q), F32)] * 3
    return pl.pallas_call(
        _attn_kernel,
        grid=grid,
        in_specs=[
            pl.BlockSpec((None, N_SLABS, TILE_ROWS, GROUP_Q), lambda g, bi, n: (bi, 0, n, g)),
            pl.BlockSpec((None, N_SLABS, TILE_ROWS, GROUP_KV), lambda g, bi, n: (bi, 0, n, g)),
            pl.BlockSpec((None, N_SLABS, TILE_ROWS, GROUP_KV),
                         lambda g, bi, n: (bi, 0, jnp.maximum(n - 1, 0), g)),
            pl.BlockSpec((3, 2, None, N_PAIRS * TILE_ROWS, 4 * TILE_ROWS),
                         lambda g, bi, n: (0, 0, g, 0, 0)),
            pl.BlockSpec((None, N_SLABS, TILE_ROWS, GROUP_Q), lambda g, bi, n: (bi, 0, n, g)),
        ],
        out_specs=pl.BlockSpec((None, N_SLABS, TILE_ROWS, GROUP_Q), lambda g, bi, n: (bi, 0, n, g)),
        out_shape=jax.ShapeDtypeStruct((b, N_SLABS, rows, D_MODEL), BF16),
        scratch_shapes=scratch,
        compiler_params=pltpu.CompilerParams(
            dimension_semantics=("arbitrary", "arbitrary", "arbitrary"),
            vmem_limit_bytes=VMEM_LIMIT),
        name="attention",
    )(q, kv, kv, bias, ag)


def _conv_kernel(cv_ref, cu_ref, cvp_ref, cup_ref, cg_ref, w_ref, b_ref, lg_ref, lb_ref, y_ref,
                 hs_ref, cs_ref):
    t = pl.program_id(1)
    tiles = CONV_ROWS // CONV_TILE
    tile_shift = tiles.bit_length() - 1

    def prep(rr, carry):
        hc = cv_ref[rr].astype(F32) * _sigmoid(cu_ref[rr].astype(F32))
        hp = cvp_ref[rr].astype(F32) * _sigmoid(cup_ref[rr].astype(F32))
        hp = jnp.where(t == 0, 0.0, hp)
        full = jnp.concatenate([hp, hc], axis=0)
        hs_ref[0, rr] = hc
        hs_ref[1, rr] = pltpu.roll(full, 1, 0)[PREV_ROWS:]
        hs_ref[2, rr] = pltpu.roll(full, 2, 0)[PREV_ROWS:]
        return carry

    lax.fori_loop(0, N_SLABS, prep, 0, unroll=2)

    def tap_body(rg, carry, lanes, r_base):
        row0 = pl.multiple_of(rg * SUBLANES, SUBLANES)
        accs = [jnp.broadcast_to(b_ref[:, lanes], (SUBLANES, LANES))] * TAP_SLABS
        for s in range(CONV_K):
            w8 = w_ref[CONV_K - 1 - s, :, lanes]
            for k in range(TAP_SLABS):
                r = r_base + k
                rr = (r - s) % N_SLABS
                down = (s - r + N_SLABS - 1) // N_SLABS
                accs[k] = accs[k] + w8 * hs_ref[down, rr, pl.ds(row0, SUBLANES), lanes]
        for k in range(TAP_SLABS):
            cs_ref[r_base + k, pl.ds(row0, SUBLANES), lanes] = accs[k]
        return carry

    for lt in range(COL_BLOCKS):
        lanes = slice(lt * LANES, (lt + 1) * LANES)
        for r_base in range(0, N_SLABS, TAP_SLABS):
            lax.fori_loop(0, CONV_ROWS // SUBLANES,
                          functools.partial(tap_body, lanes=lanes, r_base=r_base), 0)

    def ln_body(it, carry):
        r = jnp.right_shift(it, tile_shift)
        row0 = pl.multiple_of(jnp.bitwise_and(it, tiles - 1) * CONV_TILE, CONV_TILE)
        acc = cs_ref[r, pl.ds(row0, CONV_TILE), :]
        mu = jnp.mean(acc, axis=-1, keepdims=True)
        cen = acc - mu
        var = jnp.mean(cen * cen, axis=-1, keepdims=True)
        z = cen * lax.rsqrt(var + LN_EPS) * lg_ref[...] + lb_ref[...]
        gate = cg_ref[r, pl.ds(row0, CONV_TILE), :].astype(F32)
        y_ref[r, pl.ds(row0, CONV_TILE), :] = (_silu(z) * _silu(gate)).astype(y_ref.dtype)
        return carry

    lax.fori_loop(0, N_SLABS * tiles, ln_body, 0, unroll=8)


def _conv(cv, cu, cg, conv_w, conv_b, ln_g, ln_b):
    b, _, rows, _ = cv.shape
    grid = (b, rows // CONV_ROWS)
    cur = pl.BlockSpec((None, N_SLABS, CONV_ROWS, D_MODEL), lambda bi, t: (bi, 0, t, 0))
    prev = pl.BlockSpec((None, N_SLABS, PREV_ROWS, D_MODEL),
                        lambda bi, t: (bi, 0, jnp.maximum(t * (CONV_ROWS // PREV_ROWS) - 1, 0), 0))
    vec = pl.BlockSpec((1, D_MODEL), lambda bi, t: (0, 0))
    w_bcast = jnp.broadcast_to(conv_w[:, None, :], (CONV_K, SUBLANES, D_MODEL))
    return pl.pallas_call(
        _conv_kernel,
        grid=grid,
        in_specs=[cur, cur, prev, prev, cur,
                  pl.BlockSpec((CONV_K, SUBLANES, D_MODEL), lambda bi, t: (0, 0, 0)), vec, vec, vec],
        out_specs=cur,
        out_shape=jax.ShapeDtypeStruct(cv.shape, BF16),
        scratch_shapes=[pltpu.VMEM((3, N_SLABS, CONV_ROWS, D_MODEL), F32),
                        pltpu.VMEM((N_SLABS, CONV_ROWS, D_MODEL), F32)],
        compiler_params=pltpu.CompilerParams(
            dimension_semantics=("arbitrary", "arbitrary"),
            vmem_limit_bytes=VMEM_LIMIT),
        name="conv",
    )(cv, cu, cv, cu, cg, w_bcast, conv_b, ln_g, ln_b)


def _outproj_kernel(*refs):
    x_refs = refs[:COL_BLOCKS]
    ya_ref, yc_ref, wa_ref, wc_ref, g_ref, o_ref, stage = refs[COL_BLOCKS:]
    ya = ya_ref[...].reshape(N_SLABS * OUT_ROWS, D_MODEL)
    yc = yc_ref[...].reshape(N_SLABS * OUT_ROWS, D_MODEL)
    half = D_MODEL // 2
    deltas = [jnp.dot(ya, wa_ref[:, c:c + half], preferred_element_type=F32)
              + jnp.dot(yc, wc_ref[:, c:c + half], preferred_element_type=F32)
              for c in (0, half)]
    for r in range(N_SLABS):
        rows = slice(r * OUT_ROWS, (r + 1) * OUT_ROWS)
        xs = _slab_rows(x_refs, r, OUT_ROWS) + jnp.concatenate([dl[rows] for dl in deltas], axis=1)
        ms = jnp.mean(xs * xs, axis=-1, keepdims=True)
        res = xs * lax.rsqrt(ms + NORM_EPS) * g_ref[...]
        for cb in range(COL_BLOCKS):
            stage[cb, pl.ds(r, OUT_ROWS, stride=N_SLABS), :] = res[:, cb * LANES:(cb + 1) * LANES]
    for cb in range(COL_BLOCKS):
        o_ref[:, cb * LANES:(cb + 1) * LANES] = stage[cb]


def _outproj(ya, yc, x, w_att, w_conv, final_g):
    b, seq, _ = x.shape
    rows = seq // N_SLABS
    tok = OUT_ROWS * N_SLABS
    grid = (b, rows // OUT_ROWS)
    slab = pl.BlockSpec((None, N_SLABS, OUT_ROWS, D_MODEL), lambda bi, t: (bi, 0, t, 0))
    wspec = pl.BlockSpec((D_MODEL, D_MODEL), lambda bi, t: (0, 0))

    def x_spec(cb):
        return pl.BlockSpec((None, tok, LANES), lambda bi, t: (bi, t, cb))

    return pl.pallas_call(
        _outproj_kernel,
        grid=grid,
        in_specs=[x_spec(cb) for cb in range(COL_BLOCKS)] + [
            slab, slab, wspec, wspec, pl.BlockSpec((1, D_MODEL), lambda bi, t: (0, 0))],
        out_specs=pl.BlockSpec((None, tok, D_MODEL), lambda bi, t: (bi, t, 0)),
        out_shape=jax.ShapeDtypeStruct(x.shape, F32),
        scratch_shapes=[pltpu.VMEM((COL_BLOCKS, tok, LANES), F32)],
        compiler_params=pltpu.CompilerParams(
            dimension_semantics=("arbitrary", "arbitrary"),
            vmem_limit_bytes=VMEM_LIMIT),
        name="outproj",
    )(*([x] * COL_BLOCKS), ya, yc, w_att, w_conv, final_g)


def kernel(x, norm_g, w_in, conv_w, conv_b, conv_ln_g, conv_ln_b, w_out, final_norm_g):
    b, seq, dm = x.shape
    assert dm == D_MODEL and seq % (N_SLABS * TILE_ROWS) == 0
    assert norm_g.shape[0] == 1, "single layer"
    q, kv, ag, cv, cu, cg = _inproj(x, norm_g[0][None], w_in[0].astype(BF16))
    y_att = _attention(q, kv, ag, _attn_bias())
    y_conv = _conv(cv, cu, cg, conv_w[0], conv_b[0][None], conv_ln_g[0][None], conv_ln_b[0][None])
    w_o = w_out[0].astype(BF16)
    return _outproj(y_att, y_conv, x, w_o[:D_MODEL], w_o[D_MODEL:], final_norm_g[None])
```

```python
import functools
import math

import jax
import jax.numpy as jnp
import numpy as np
from jax import lax
from jax.experimental import pallas as pl
from jax.experimental.pallas import tpu as pltpu

F32 = jnp.float32
BF16 = jnp.bfloat16

D_MODEL = 1024
HEAD_DIM = 64
N_Q_HEADS = 16
N_KV_HEADS = 4
Q_PER_KV = N_Q_HEADS // N_KV_HEADS
KV_WIDTH = N_KV_HEADS * HEAD_DIM
CONV_K = 31
WINDOW_STEPS = 128
NORM_EPS = 1e-6
LN_EPS = 1e-5
LOG2E = math.log2(math.e)

LANES = 128
SUBLANES = 8
N_SLABS = 16
TILE_ROWS = 128
PATTERNS = ((1, 16, 8), (4, 4, 32), (16, 1, 128))
BLOCKS_PER_PATTERN = 16
GROUP_Q = Q_PER_KV * HEAD_DIM
GROUP_KV = 2 * HEAD_DIM
PAIR = 2 * HEAD_DIM
N_PAIRS = Q_PER_KV // 2
COL_BLOCKS = D_MODEL // LANES

IN_ROWS = 32
CONV_ROWS = 64
CONV_TILE = 16
TAP_SLABS = 16
PREV_ROWS = 16
OUT_ROWS = 64
VMEM_LIMIT = 56 * 1024 * 1024


def _sigmoid(x):
    return 1.0 / (1.0 + jnp.exp(-x))


def _silu(x):
    return x * _sigmoid(x)


def _slab_rows(x_refs, r, rows):
    return jnp.concatenate([xr[pl.ds(r, rows, stride=N_SLABS), :] for xr in x_refs], axis=-1)


def _inproj_kernel(*refs):
    x_refs = refs[:COL_BLOCKS]
    ng_ref, w_ref = refs[COL_BLOCKS:COL_BLOCKS + 2]
    q_ref, kv_ref, ag_ref, cv_ref, cu_ref, cg_ref = refs[COL_BLOCKS + 2:]
    hs = []
    for r in range(N_SLABS):
        xs = _slab_rows(x_refs, r, IN_ROWS)
        ms = jnp.mean(xs * xs, axis=-1, keepdims=True)
        hs.append((xs * lax.rsqrt(ms + NORM_EPS) * ng_ref[...]).astype(BF16))
    h = jnp.concatenate(hs, axis=0)

    def emit(ref, c0, width, scale):
        for cc in range(0, width, 512):
            wc = min(512, width - cc)
            res = jnp.dot(h, w_ref[:, c0 + cc:c0 + cc + wc], preferred_element_type=F32)
            if scale is not None:
                res = res * scale
            for r in range(N_SLABS):
                ref[r, :, cc:cc + wc] = res[r * IN_ROWS:(r + 1) * IN_ROWS].astype(ref.dtype)

    c = 0
    emit(q_ref, c, D_MODEL, HEAD_DIM ** -0.5 * LOG2E)
    c += D_MODEL
    res_k = jnp.dot(h, w_ref[:, c:c + KV_WIDTH], preferred_element_type=F32)
    res_v = jnp.dot(h, w_ref[:, c + KV_WIDTH:c + 2 * KV_WIDTH], preferred_element_type=F32)
    heads = [res[:, g * HEAD_DIM:(g + 1) * HEAD_DIM] for g in range(N_KV_HEADS) for res in (res_k, res_v)]
    res_kv = jnp.concatenate(heads, axis=1)
    for r in range(N_SLABS):
        kv_ref[r] = res_kv[r * IN_ROWS:(r + 1) * IN_ROWS]
    c += 2 * KV_WIDTH
    for ref in (ag_ref, cv_ref, cu_ref, cg_ref):
        emit(ref, c, D_MODEL, None)
        c += D_MODEL


def _inproj(x, norm_g, w_perm):
    b, seq, _ = x.shape
    rows = seq // N_SLABS
    grid = (b, rows // IN_ROWS)

    def slab_out(width, dtype):
        return (jax.ShapeDtypeStruct((b, N_SLABS, rows, width), dtype),
                pl.BlockSpec((None, N_SLABS, IN_ROWS, width), lambda bi, j: (bi, 0, j, 0)))

    def x_spec(cb):
        return pl.BlockSpec((None, IN_ROWS * N_SLABS, LANES), lambda bi, j: (bi, j, cb))

    outs = [slab_out(D_MODEL, F32), slab_out(2 * KV_WIDTH, F32)] + [slab_out(D_MODEL, BF16)] * 4
    return pl.pallas_call(
        _inproj_kernel,
        grid=grid,
        in_specs=[x_spec(cb) for cb in range(COL_BLOCKS)] + [
            pl.BlockSpec((1, D_MODEL), lambda bi, j: (0, 0)),
            pl.BlockSpec(w_perm.shape, lambda bi, j: (0, 0))],
        out_specs=[o[1] for o in outs],
        out_shape=[o[0] for o in outs],
        compiler_params=pltpu.CompilerParams(
            dimension_semantics=("arbitrary", "arbitrary"),
            vmem_limit_bytes=VMEM_LIMIT),
        name="inproj",
    )(*([x] * COL_BLOCKS), norm_g, w_perm)


def _attn_bias():
    slopes = np.exp2(-8.0 * (np.arange(N_Q_HEADS, dtype=np.float32) + 1.0) / N_Q_HEADS)
    out = []
    for d, nm, c in PATTERNS:
        a = np.arange(TILE_ROWS)
        jq = (a % c) * nm + a // c
        col = np.arange(2 * TILE_ROWS)
        part, rem = col // TILE_ROWS, col % TILE_ROWS
        jk = (rem % c) * nm + rem // c - (1 - part) * TILE_ROWS
        dist = jq[:, None] - jk[None, :]
        valid = (dist >= 0) & (dist <= WINDOW_STEPS)
        bias = -slopes[:, None, None] * (dist * d).astype(np.float32)[None] * np.float32(LOG2E)
        v0 = np.where(valid[None], bias, -np.inf)
        v1 = np.where((valid & (part[None, :] == 1))[None], bias, -np.inf)
        both = np.stack([v0, v1])
        both = both.reshape(2, N_KV_HEADS, N_PAIRS, 2, TILE_ROWS, 2 * TILE_ROWS)
        both = both.transpose(0, 1, 2, 4, 3, 5)
        out.append(both.reshape(2, N_KV_HEADS, N_PAIRS * TILE_ROWS, 4 * TILE_ROWS))
    return jnp.asarray(np.stack(out), F32)


def _attn_kernel(q_ref, kvc_ref, kvp_ref, bias_ref, ag_ref, y_ref, kvr, n_scr, d_scr, m_scr):
    first_tile = (pl.program_id(2) == 0).astype(jnp.int32)
    lo = lax.broadcasted_iota(jnp.int32, (1, PAIR), 1) < HEAD_DIM
    ones_top = jnp.broadcast_to(jnp.where(lo, 1.0, 0.0), (2 * TILE_ROWS, PAIR))
    ones_rhs = jnp.concatenate([ones_top, 1.0 - ones_top], axis=0).astype(BF16)

    for s in range(N_SLABS):
        kvr[s, :TILE_ROWS] = pltpu.roll(kvp_ref[s], HEAD_DIM, 1)
        kvr[s, TILE_ROWS:] = pltpu.roll(kvc_ref[s], HEAD_DIM, 1)

    def block(p, rd, u):
        d, nm, c = PATTERNS[p]
        slabs = [rd + d * m for m in range(nm)]
        r0 = u * c
        qs = jnp.concatenate([q_ref[sl, r0:r0 + c, pr * PAIR:(pr + 1) * PAIR]
                              for pr in range(N_PAIRS) for sl in slabs], axis=0).astype(BF16)
        if u == 0:
            prev_a = [kvp_ref[sl, TILE_ROWS - c:TILE_ROWS, :] for sl in slabs]
        else:
            prev_a = [kvc_ref[sl, r0 - c:r0, :] for sl in slabs]
        kv_a = jnp.concatenate(prev_a + [kvc_ref[sl, r0:r0 + c, :] for sl in slabs], axis=0)
        kv_b = jnp.concatenate([kvr[sl, TILE_ROWS + r0 - c:TILE_ROWS + r0, :] for sl in slabs]
                               + [kvr[sl, TILE_ROWS + r0:TILE_ROWS + r0 + c, :] for sl in slabs],
                               axis=0)
        k_nt = jnp.concatenate([jnp.where(lo, kv_a, 0.0), jnp.where(lo, 0.0, kv_b)],
                               axis=0).astype(BF16)
        v_bd = jnp.concatenate([jnp.where(lo, kv_b, 0.0), jnp.where(lo, 0.0, kv_a)],
                               axis=0).astype(BF16)
        s = lax.dot_general(qs, k_nt, (((1,), (1,)), ((), ())), preferred_element_type=F32)
        s = s + (bias_ref[p, first_tile] if u == 0 else bias_ref[p, 0])
        sa, sb = s[:, :2 * TILE_ROWS], s[:, 2 * TILE_ROWS:]
        ma = jnp.max(sa, axis=-1, keepdims=True)
        mb = jnp.max(sb, axis=-1, keepdims=True)
        pe = jnp.concatenate([jnp.exp2(sa - ma), jnp.exp2(sb - mb)], axis=1).astype(BF16)
        rhs = jnp.concatenate([v_bd, ones_rhs], axis=1)
        r = jnp.dot(pe, rhs, preferred_element_type=F32)
        return r[:, :PAIR], r[:, PAIR:], jnp.where(lo, ma, mb)

    def merge(old, new):
        (n0, d0, m0), (n1, d1, m1) = old, new
        mx = jnp.maximum(m0, m1)
        e0, e1 = jnp.exp2(m0 - mx), jnp.exp2(m1 - mx)
        return e0 * n0 + e1 * n1, e0 * d0 + e1 * d1, mx

    scrs = (n_scr, d_scr, m_scr)
    for p in range(len(PATTERNS) - 1):
        d, nm, c = PATTERNS[p]
        for blk in range(BLOCKS_PER_PATTERN):
            rd, u = blk // nm, blk % nm
            parts = block(p, rd, u)
            where = [(rd + d * m, slice(u * c, (u + 1) * c), slice(pr * PAIR, (pr + 1) * PAIR))
                     for pr in range(N_PAIRS) for m in range(nm)]
            if p > 0:
                old = [jnp.concatenate([scr[w] for w in where], axis=0) for scr in scrs]
                parts = merge(old, parts)
            for k, w in enumerate(where):
                for scr, val in zip(scrs, parts):
                    scr[w] = val[k * c:(k + 1) * c]

    p_last = len(PATTERNS) - 1
    assert PATTERNS[p_last][1:] == (1, TILE_ROWS)
    for slab in range(N_SLABS):
        new = [jnp.concatenate([v[:TILE_ROWS], v[TILE_ROWS:]], axis=1)
               for v in block(p_last, slab, 0)]
        num, den, _ = merge([scr[slab] for scr in scrs], new)
        gate = ag_ref[slab].astype(F32)
        y_ref[slab] = (num / den * _silu(gate)).astype(y_ref.dtype)


def _attention(q, kv, ag, bias):
    b, _, rows, _ = q.shape
    n_tiles = rows // TILE_ROWS
    grid = (N_KV_HEADS, b, n_tiles)
    scratch = [pltpu.VMEM((N_SLABS, 2 * TILE_ROWS, GROUP_KV), F32)]
    scratch += [pltpu.VMEM((N_SLABS, TILE_ROWS, GROUP_Q), F32)] * 3
    return pl.pallas_call(
        _attn_kernel,
        grid=grid,
        in_specs=[
            pl.BlockSpec((None, N_SLABS, TILE_ROWS, GROUP_Q), lambda g, bi, n: (bi, 0, n, g)),
            pl.BlockSpec((None, N_SLABS, TILE_ROWS, GROUP_KV), lambda g, bi, n: (bi, 0, n, g)),
            pl.BlockSpec((None, N_SLABS, TILE_ROWS, GROUP_KV),
                         lambda g, bi, n: (bi, 0, jnp.maximum(n - 1, 0), g)),
            pl.BlockSpec((3, 2, None, N_PAIRS * TILE_ROWS, 4 * TILE_ROWS),
                         lambda g, bi, n: (0, 0, g, 0, 0)),
            pl.BlockSpec((None, N_SLABS, TILE_ROWS, GROUP_Q), lambda g, bi, n: (bi, 0, n, g)),
        ],
        out_specs=pl.BlockSpec((None, N_SLABS, TILE_ROWS, GROUP_Q), lambda g, bi, n: (bi, 0, n, g)),
        out_shape=jax.ShapeDtypeStruct((b, N_SLABS, rows, D_MODEL), BF16),
        scratch_shapes=scratch,
        compiler_params=pltpu.CompilerParams(
            dimension_semantics=("arbitrary", "arbitrary", "arbitrary"),
            vmem_limit_bytes=VMEM_LIMIT),
        name="attention",
    )(q, kv, kv, bias, ag)


def _conv_kernel(cv_ref, cu_ref, cvp_ref, cup_ref, cg_ref, w_ref, b_ref, lg_ref, lb_ref, y_ref,
                 hs_ref, cs_ref):
    t = pl.program_id(1)
    tiles = CONV_ROWS // CONV_TILE
    tile_shift = tiles.bit_length() - 1

    def prep(rr, carry):
        hc = cv_ref[rr].astype(F32) * _sigmoid(cu_ref[rr].astype(F32))
        hp = cvp_ref[rr].astype(F32) * _sigmoid(cup_ref[rr].astype(F32))
        hp = jnp.where(t == 0, 0.0, hp)
        full = jnp.concatenate([hp, hc], axis=0)
        hs_ref[0, rr] = hc
        hs_ref[1, rr] = pltpu.roll(full, 1, 0)[PREV_ROWS:]
        hs_ref[2, rr] = pltpu.roll(full, 2, 0)[PREV_ROWS:]
        return carry

    lax.fori_loop(0, N_SLABS, prep, 0, unroll=2)

    def tap_body(rg, carry, lanes, r_base):
        row0 = pl.multiple_of(rg * SUBLANES, SUBLANES)
        accs = [jnp.broadcast_to(b_ref[:, lanes], (SUBLANES, LANES))] * TAP_SLABS
        for s in range(CONV_K):
            w8 = w_ref[CONV_K - 1 - s, :, lanes]
            for k in range(TAP_SLABS):
                r = r_base + k
                rr = (r - s) % N_SLABS
                down = (s - r + N_SLABS - 1) // N_SLABS
                accs[k] = accs[k] + w8 * hs_ref[down, rr, pl.ds(row0, SUBLANES), lanes]
        for k in range(TAP_SLABS):
            cs_ref[r_base + k, pl.ds(row0, SUBLANES), lanes] = accs[k]
        return carry

    for lt in range(COL_BLOCKS):
        lanes = slice(lt * LANES, (lt + 1) * LANES)
        for r_base in range(0, N_SLABS, TAP_SLABS):
            lax.fori_loop(0, CONV_ROWS // SUBLANES,
                          functools.partial(tap_body, lanes=lanes, r_base=r_base), 0, unroll=2)

    def ln_body(it, carry):
        r = jnp.right_shift(it, tile_shift)
        row0 = pl.multiple_of(jnp.bitwise_and(it, tiles - 1) * CONV_TILE, CONV_TILE)
        acc = cs_ref[r, pl.ds(row0, CONV_TILE), :]
        mu = jnp.mean(acc, axis=-1, keepdims=True)
        cen = acc - mu
        var = jnp.mean(cen * cen, axis=-1, keepdims=True)
        z = cen * lax.rsqrt(var + LN_EPS) * lg_ref[...] + lb_ref[...]
        gate = cg_ref[r, pl.ds(row0, CONV_TILE), :].astype(F32)
        y_ref[r, pl.ds(row0, CONV_TILE), :] = (_silu(z) * _silu(gate)).astype(y_ref.dtype)
        return carry

    lax.fori_loop(0, N_SLABS * tiles, ln_body, 0, unroll=8)


def _conv(cv, cu, cg, conv_w, conv_b, ln_g, ln_b):
    b, _, rows, _ = cv.shape
    grid = (b, rows // CONV_ROWS)
    cur = pl.BlockSpec((None, N_SLABS, CONV_ROWS, D_MODEL), lambda bi, t: (bi, 0, t, 0))
    prev = pl.BlockSpec((None, N_SLABS, PREV_ROWS, D_MODEL),
                        lambda bi, t: (bi, 0, jnp.maximum(t * (CONV_ROWS // PREV_ROWS) - 1, 0), 0))
    vec = pl.BlockSpec((1, D_MODEL), lambda bi, t: (0, 0))
    w_bcast = jnp.broadcast_to(conv_w[:, None, :], (CONV_K, SUBLANES, D_MODEL))
    return pl.pallas_call(
        _conv_kernel,
        grid=grid,
        in_specs=[cur, cur, prev, prev, cur,
                  pl.BlockSpec((CONV_K, SUBLANES, D_MODEL), lambda bi, t: (0, 0, 0)), vec, vec, vec],
        out_specs=cur,
        out_shape=jax.ShapeDtypeStruct(cv.shape, BF16),
        scratch_shapes=[pltpu.VMEM((3, N_SLABS, CONV_ROWS, D_MODEL), F32),
                        pltpu.VMEM((N_SLABS, CONV_ROWS, D_MODEL), F32)],
        compiler_params=pltpu.CompilerParams(
            dimension_semantics=("arbitrary", "arbitrary"),
            vmem_limit_bytes=VMEM_LIMIT),
        name="conv",
    )(cv, cu, cv, cu, cg, w_bcast, conv_b, ln_g, ln_b)


def _outproj_kernel(*refs):
    x_refs = refs[:COL_BLOCKS]
    ya_ref, yc_ref, wa_ref, wc_ref, g_ref, o_ref, stage = refs[COL_BLOCKS:]
    ya = ya_ref[...].reshape(N_SLABS * OUT_ROWS, D_MODEL)
    yc = yc_ref[...].reshape(N_SLABS * OUT_ROWS, D_MODEL)
    half = D_MODEL // 2
    deltas = [jnp.dot(ya, wa_ref[:, c:c + half], preferred_element_type=F32)
              + jnp.dot(yc, wc_ref[:, c:c + half], preferred_element_type=F32)
              for c in (0, half)]
    for r in range(N_SLABS):
        rows = slice(r * OUT_ROWS, (r + 1) * OUT_ROWS)
        xs = _slab_rows(x_refs, r, OUT_ROWS) + jnp.concatenate([dl[rows] for dl in deltas], axis=1)
        ms = jnp.mean(xs * xs, axis=-1, keepdims=True)
        res = xs * lax.rsqrt(ms + NORM_EPS) * g_ref[...]
        for cb in range(COL_BLOCKS):
            stage[cb, pl.ds(r, OUT_ROWS, stride=N_SLABS), :] = res[:, cb * LANES:(cb + 1) * LANES]
    for cb in range(COL_BLOCKS):
        o_ref[:, cb * LANES:(cb + 1) * LANES] = stage[cb]


def _outproj(ya, yc, x, w_att, w_conv, final_g):
    b, seq, _ = x.shape
    rows = seq // N_SLABS
    tok = OUT_ROWS * N_SLABS
    grid = (b, rows // OUT_ROWS)
    slab = pl.BlockSpec((None, N_SLABS, OUT_ROWS, D_MODEL), lambda bi, t: (bi, 0, t, 0))
    wspec = pl.BlockSpec((D_MODEL, D_MODEL), lambda bi, t: (0, 0))

    def x_spec(cb):
        return pl.BlockSpec((None, tok, LANES), lambda bi, t: (bi, t, cb))

    return pl.pallas_call(
        _outproj_kernel,
        grid=grid,
        in_specs=[x_spec(cb) for cb in range(COL_BLOCKS)] + [
            slab, slab, wspec, wspec, pl.BlockSpec((1, D_MODEL), lambda bi, t: (0, 0))],
        out_specs=pl.BlockSpec((None, tok, D_MODEL), lambda bi, t: (bi, t, 0)),
        out_shape=jax.ShapeDtypeStruct(x.shape, F32),
        scratch_shapes=[pltpu.VMEM((COL_BLOCKS, tok, LANES), F32)],
        compiler_params=pltpu.CompilerParams(
            dimension_semantics=("arbitrary", "arbitrary"),
            vmem_limit_bytes=VMEM_LIMIT),
        name="outproj",
    )(*([x] * COL_BLOCKS), ya, yc, w_att, w_conv, final_g)


def kernel(x, norm_g, w_in, conv_w, conv_b, conv_ln_g, conv_ln_b, w_out, final_norm_g):
    b, seq, dm = x.shape
    assert dm == D_MODEL and seq % (N_SLABS * TILE_ROWS) == 0
    assert norm_g.shape[0] == 1, "single layer"
    q, kv, ag, cv, cu, cg = _inproj(x, norm_g[0][None], w_in[0].astype(BF16))
    y_att = _attention(q, kv, ag, _attn_bias())
    y_conv = _conv(cv, cu, cg, conv_w[0], conv_b[0][None], conv_ln_g[0][None], conv_ln_b[0][None])
    w_o = w_out[0].astype(BF16)
    return _outproj(y_att, y_conv, x, w_o[:D_MODEL], w_o[D_MODEL:], final_norm_g[None])
```
